```python
import math
import jax, jax.numpy as jnp
from jax import lax
import numpy as np

D_MODEL = 1024
BATCH = 8
SEQ = 2048
DEPTH = 4

ATTN_HEADS = 6
ATTN_HEAD_DIM = 64
ATTN_WIDTH = ATTN_HEADS * ATTN_HEAD_DIM
MOBA_BLOCK = 256
MOBA_TOPK = 3
MOBA_QUERY_CHUNK = 64
SSD_HEADS = 6
SSD_HEAD_DIM = 64
SSD_WIDTH = SSD_HEADS * SSD_HEAD_DIM
SSD_GROUPS = 2
SSD_STATE = 128
SSD_CONV = 4
SSD_CHUNK = 128
XBC_WIDTH = SSD_WIDTH + 2 * SSD_GROUPS * SSD_STATE
POOL_WINDOWS = (2, 4, 8, 16)
POOL_GROUPS = len(POOL_WINDOWS)
POOL_GROUP_DIM = 64
POOL_WIDTH = POOL_GROUPS * POOL_GROUP_DIM
MIX_WIDTH = ATTN_WIDTH + SSD_WIDTH + POOL_WIDTH
IN_PROJ_WIDTH = 3 * ATTN_WIDTH + SSD_WIDTH + XBC_WIDTH + SSD_HEADS + POOL_WIDTH
FFN_HIDDEN = ((8 * D_MODEL // 3 + 255) // 256) * 256
NORM_EPS = 1e-6
NEG_INF = -1e30

kernel_name = 'hybrid_moba_ssd_pool_trunk'


def rmsnorm(x, g):
    xf = x.astype(jnp.float32)
    xf = xf * lax.rsqrt(jnp.mean(xf * xf, axis=-1, keepdims=True) + NORM_EPS)
    return xf.astype(x.dtype) * g


def causal_depthwise_conv(x, w, b):
    k_width, ch = w.shape
    y = lax.conv_general_dilated(
        x, w[:, None, :], window_strides=(1,), padding=[(k_width - 1, 0)],
        dimension_numbers=('NWC', 'WIO', 'NWC'), feature_group_count=ch)
    return y + b


def moba_attention(q, k, v):
    bsz, n_heads, seq, hd = q.shape
    n_blocks = -(-seq // MOBA_BLOCK)
    pad = n_blocks * MOBA_BLOCK - seq
    k_p = jnp.pad(k, ((0, 0), (0, 0), (0, pad), (0, 0)))
    v_p = jnp.pad(v, ((0, 0), (0, 0), (0, pad), (0, 0)))
    k_blocks = k_p.reshape(bsz, n_heads, n_blocks, MOBA_BLOCK, hd)
    v_blocks = v_p.reshape(bsz, n_heads, n_blocks, MOBA_BLOCK, hd)
    k_mean = jnp.mean(k_blocks, axis=3)
    n_sel = min(MOBA_TOPK, n_blocks)
    scale = hd ** -0.5
    b_idx = jnp.arange(bsz)[:, None, None, None]
    h_idx = jnp.arange(n_heads)[None, :, None, None]
    block_ids = jnp.arange(n_blocks)

    def query_chunk(c):
        q0 = c * MOBA_QUERY_CHUNK
        qc = lax.dynamic_slice_in_dim(q, q0, MOBA_QUERY_CHUNK, axis=2)
        blk = q0 // MOBA_BLOCK
        gate = jnp.einsum('bhqd,bhnd->bhqn', qc, k_mean).astype(jnp.float32)
        gate = jnp.where(block_ids < blk, gate, NEG_INF)
        _, idx = lax.top_k(gate, n_sel)
        valid = jnp.arange(n_sel) < blk
        k_sel = k_blocks[b_idx, h_idx, idx]
        v_sel = v_blocks[b_idx, h_idx, idx]
        s_sel = jnp.einsum('bhqd,bhqnkd->bhqnk', qc, k_sel).astype(jnp.float32) * scale
        s_sel = jnp.where(valid[:, None], s_sel, NEG_INF)
        s_sel = s_sel.reshape(bsz, n_heads, MOBA_QUERY_CHUNK, n_sel * MOBA_BLOCK)
        k_own = lax.dynamic_index_in_dim(k_blocks, blk, axis=2, keepdims=False)
        v_own = lax.dynamic_index_in_dim(v_blocks, blk, axis=2, keepdims=False)
        s_own = jnp.einsum('bhqd,bhkd->bhqk', qc, k_own).astype(jnp.float32) * scale
        q_pos = q0 + jnp.arange(MOBA_QUERY_CHUNK)
        k_pos = blk * MOBA_BLOCK + jnp.arange(MOBA_BLOCK)
        s_own = jnp.where(k_pos[None, :] <= q_pos[:, None], s_own, NEG_INF)
        p = jax.nn.softmax(jnp.concatenate([s_sel, s_own], axis=-1), axis=-1).astype(v.dtype)
        p_sel = p[..., :n_sel * MOBA_BLOCK].reshape(bsz, n_heads, MOBA_QUERY_CHUNK, n_sel, MOBA_BLOCK)
        p_own = p[..., n_sel * MOBA_BLOCK:]
        return (jnp.einsum('bhqnk,bhqnkd->bhqd', p_sel, v_sel)
                + jnp.einsum('bhqk,bhkd->bhqd', p_own, v_own))

    out = lax.map(query_chunk, jnp.arange(seq // MOBA_QUERY_CHUNK))
    return out.transpose(1, 0, 3, 2, 4).reshape(bsz, seq, n_heads * hd)


def ssd_mixer(xbc, z, dt_raw, conv_w, conv_b, dt_bias, a_log, d_skip, norm_w):
    bsz, seq, _ = xbc.shape
    n_chunks = seq // SSD_CHUNK
    rep = SSD_HEADS // SSD_GROUPS
    xbc = jax.nn.silu(causal_depthwise_conv(xbc, conv_w, conv_b)).astype(jnp.float32)
    xs, b_in, c_in = jnp.split(xbc, [SSD_WIDTH, SSD_WIDTH + SSD_GROUPS * SSD_STATE], axis=-1)
    x_h = xs.reshape(bsz, seq, SSD_HEADS, SSD_HEAD_DIM)
    b_h = jnp.repeat(b_in.reshape(bsz, seq, SSD_GROUPS, SSD_STATE), rep, axis=2)
    c_h = jnp.repeat(c_in.reshape(bsz, seq, SSD_GROUPS, SSD_STATE), rep, axis=2)
    dt = jax.nn.softplus(dt_raw.astype(jnp.float32) + dt_bias.astype(jnp.float32))
    a = -jnp.exp(a_log.astype(jnp.float32))

    def chunks(t):
        return t.reshape((bsz, n_chunks, SSD_CHUNK) + t.shape[2:])

    xdt = chunks(x_h * dt[..., None])
    b_c, c_c = chunks(b_h), chunks(c_h)
    a_cum = jnp.cumsum(chunks(dt * a).transpose(0, 3, 1, 2), axis=-1)
    causal = jnp.tril(jnp.ones((SSD_CHUNK, SSD_CHUNK), dtype=bool))
    decay = jnp.exp(jnp.where(causal, a_cum[..., :, None] - a_cum[..., None, :], -jnp.inf))
    y_diag = jnp.einsum('bclhn,bcshn,bhcls,bcshp->bclhp', c_c, b_c, decay, xdt)
    decay_to_end = jnp.exp(a_cum[..., -1:] - a_cum)
    states = jnp.einsum('bclhn,bhcl,bclhp->bchpn', b_c, decay_to_end, xdt)
    chunk_decay = jnp.exp(a_cum[..., -1])

    def step(h, inp):
        st, d = inp
        return h * d[..., None, None] + st, h

    _, prev = lax.scan(step, jnp.zeros_like(states[:, 0]),
                       (states.transpose(1, 0, 2, 3, 4), chunk_decay.transpose(2, 0, 1)))
    prev = prev.transpose(1, 0, 2, 3, 4)
    y_off = jnp.einsum('bclhn,bchpn,bhcl->bclhp', c_c, prev, jnp.exp(a_cum))
    y = (y_diag + y_off).reshape(bsz, seq, SSD_HEADS, SSD_HEAD_DIM) + x_h * d_skip.astype(jnp.float32)[:, None]
    y = y.reshape(bsz, seq, SSD_WIDTH) * jax.nn.silu(z.astype(jnp.float32))
    yg = y.reshape(bsz, seq, SSD_GROUPS, SSD_WIDTH // SSD_GROUPS)
    yg = yg * lax.rsqrt(jnp.mean(yg * yg, axis=-1, keepdims=True) + NORM_EPS)
    return (yg.reshape(bsz, seq, SSD_WIDTH) * norm_w.astype(jnp.float32)).astype(z.dtype)


def pool_mixer(p, pool_w, pool_scale):
    bsz, seq, _ = p.shape
    pg = p.astype(jnp.float32).reshape(bsz, seq, POOL_GROUPS, POOL_GROUP_DIM)
    csum = jnp.concatenate([jnp.zeros_like(pg[:, :1]), jnp.cumsum(pg, axis=1)], axis=1)
    t = jnp.arange(seq)[:, None]
    win = jnp.asarray(POOL_WINDOWS, dtype=jnp.int32)[None, :]
    start = jnp.maximum(t + 1 - win, 0)
    lower = csum[:, start, jnp.arange(POOL_GROUPS)[None, :], :]
    count = jnp.minimum(t + 1, win).astype(jnp.float32)
    mean = (csum[:, 1:] - lower) / count[None, :, :, None]
    mixed = jnp.einsum('bsgc,gcd->bsgd', mean - pg, pool_w.astype(jnp.float32))
    return (mixed.reshape(bsz, seq, POOL_WIDTH) * pool_scale.astype(jnp.float32)).astype(p.dtype)


def hybrid_layer(x, norm_mix, w_in, conv_w, conv_b, dt_bias, a_log, d_skip, ssd_norm,
                 pool_w, pool_scale, w_out, norm_ffn, w_gate_up, w_down):
    bsz, seq, _ = x.shape
    h = rmsnorm(x, norm_mix)
    u = h @ w_in
    sizes = (ATTN_WIDTH, ATTN_WIDTH, ATTN_WIDTH, SSD_WIDTH, XBC_WIDTH, SSD_HEADS, POOL_WIDTH)
    offsets = []
    acc = 0
    for s in sizes[:-1]:
        acc += s
        offsets.append(acc)
    q, k, v, z, xbc, dt_raw, p_in = jnp.split(u, offsets, axis=-1)

    def heads(t):
        return t.reshape(bsz, seq, ATTN_HEADS, ATTN_HEAD_DIM).transpose(0, 2, 1, 3)

    y_attn = moba_attention(heads(q), heads(k), heads(v))
    y_ssd = ssd_mixer(xbc, z, dt_raw, conv_w, conv_b, dt_bias, a_log, d_skip, ssd_norm)
    y_pool = pool_mixer(p_in, pool_w, pool_scale)
    x = x + jnp.concatenate([y_attn, y_ssd, y_pool], axis=-1) @ w_out
    h = rmsnorm(x, norm_ffn)
    gate, up = jnp.split(h @ w_gate_up, 2, axis=-1)
    return x + (jax.nn.silu(gate) * up) @ w_down


def setup_inputs(seed: int = 0) -> dict:
    key = jax.random.key(seed)
    ks = jax.random.split(key, 16)
    nrm = jax.random.normal
    dt0 = jnp.exp(jax.random.uniform(ks[5], (DEPTH, SSD_HEADS), minval=math.log(1e-3), maxval=math.log(1e-1)))
    return {
        'x': nrm(ks[0], (BATCH, SEQ, D_MODEL), jnp.float32),
        'norm_mix': 1.0 + 0.05 * nrm(ks[1], (DEPTH, D_MODEL), jnp.float32),
        'w_in': nrm(ks[2], (DEPTH, D_MODEL, IN_PROJ_WIDTH), jnp.float32) * D_MODEL ** -0.5,
        'conv_w': nrm(ks[3], (DEPTH, SSD_CONV, XBC_WIDTH), jnp.float32) * SSD_CONV ** -0.5,
        'conv_b': 0.02 * nrm(ks[4], (DEPTH, XBC_WIDTH), jnp.float32),
        'dt_bias': dt0 + jnp.log(-jnp.expm1(-dt0)),
        'a_log': jnp.log(jax.random.uniform(ks[6], (DEPTH, SSD_HEADS), minval=1.0, maxval=16.0)),
        'd_skip': 1.0 + 0.1 * nrm(ks[7], (DEPTH, SSD_HEADS), jnp.float32),
        'ssd_norm': 1.0 + 0.05 * nrm(ks[8], (DEPTH, SSD_WIDTH), jnp.float32),
        'pool_w': nrm(ks[9], (DEPTH, POOL_GROUPS, POOL_GROUP_DIM, POOL_GROUP_DIM), jnp.float32) * POOL_GROUP_DIM ** -0.5,
        'pool_scale': 1.0 + 0.1 * nrm(ks[10], (DEPTH, POOL_WIDTH), jnp.float32),
        'w_out': nrm(ks[11], (DEPTH, MIX_WIDTH, D_MODEL), jnp.float32) * MIX_WIDTH ** -0.5,
        'norm_ffn': 1.0 + 0.05 * nrm(ks[12], (DEPTH, D_MODEL), jnp.float32),
        'w_gate_up': nrm(ks[13], (DEPTH, D_MODEL, 2 * FFN_HIDDEN), jnp.float32) * D_MODEL ** -0.5,
        'w_down': nrm(ks[14], (DEPTH, FFN_HIDDEN, D_MODEL), jnp.float32) * FFN_HIDDEN ** -0.5,
        'norm_final': 1.0 + 0.05 * nrm(ks[15], (D_MODEL,), jnp.float32),
    }


def reference(x, norm_mix, w_in, conv_w, conv_b, dt_bias, a_log, d_skip, ssd_norm,
              pool_w, pool_scale, w_out, norm_ffn, w_gate_up, w_down, norm_final):
    for l in range(DEPTH):
        x = hybrid_layer(x, norm_mix[l], w_in[l], conv_w[l], conv_b[l], dt_bias[l], a_log[l],
                         d_skip[l], ssd_norm[l], pool_w[l], pool_scale[l], w_out[l],
                         norm_ffn[l], w_gate_up[l], w_down[l])
    return rmsnorm(x, norm_final)
```

```python
import functools
import math

import jax
import jax.numpy as jnp
from jax import lax
from jax.experimental import pallas as pl
from jax.experimental.pallas import tpu as pltpu

F32 = jnp.float32
BF16 = jnp.bfloat16

D_MODEL = 1024
ATTN_HEADS = 6
HEAD_DIM = 64
ATTN_WIDTH = ATTN_HEADS * HEAD_DIM
MOBA_BLOCK = 256
MOBA_TOPK = 3
SSD_HEADS = 6
SSD_WIDTH = SSD_HEADS * HEAD_DIM
SSD_GROUPS = 2
SSD_STATE = 128
SSD_CONV = 4
SSD_CHUNK = 128
XBC_WIDTH = SSD_WIDTH + 2 * SSD_GROUPS * SSD_STATE
POOL_WINDOWS = (2, 4, 8, 16)
POOL_WIDTH = 256
MIX_WIDTH = 1024
FFN_HIDDEN = 2816
NORM_EPS = 1e-6
NEG_INF = -1e30

LANES = 128
SUBLANES = 8
QKV_W = 3 * ATTN_WIDTH
DT_PAD = LANES
COL_Z = QKV_W
COL_XBC = COL_Z + SSD_WIDTH
COL_DT = COL_XBC + XBC_WIDTH
COL_P = COL_DT + DT_PAD
IN_PROJ_PAD = COL_P + POOL_WIDTH

ROW_TILE = 512
FFN_CHUNK = 256
VMEM_LIMIT = 56 * 1024 * 1024


def _resident(shape):
    nd = len(shape)
    return pl.BlockSpec(shape, lambda *_: (0,) * nd, pipeline_mode=pl.Buffered(1))


def _rmsnorm(x, g):
    ms = jnp.mean(x * x, axis=-1, keepdims=True)
    return x * lax.rsqrt(ms + NORM_EPS) * g


def _in_proj_kernel(x_ref, g_ref, w_ref, qkv_ref, z_ref, xbc_ref, dt_ref, p_ref):
    h = _rmsnorm(x_ref[...], g_ref[...]).astype(BF16)
    for ref, lo, hi in ((qkv_ref, 0, COL_Z), (z_ref, COL_Z, COL_XBC), (xbc_ref, COL_XBC, COL_DT),
                        (dt_ref, COL_DT, COL_P), (p_ref, COL_P, IN_PROJ_PAD)):
        ref[...] = jnp.dot(h, w_ref[:, lo:hi], preferred_element_type=F32)


def _in_proj(x2, g, w):
    m = x2.shape[0]
    widths = (QKV_W, SSD_WIDTH, XBC_WIDTH, DT_PAD, POOL_WIDTH)
    return pl.pallas_call(
        _in_proj_kernel,
        grid=(m // ROW_TILE,),
        in_specs=[pl.BlockSpec((ROW_TILE, D_MODEL), lambda i: (i, 0)),
                  _resident((1, D_MODEL)),
                  _resident((D_MODEL, IN_PROJ_PAD))],
        out_specs=[pl.BlockSpec((ROW_TILE, wd), lambda i: (i, 0)) for wd in widths],
        out_shape=[jax.ShapeDtypeStruct((m, wd), F32) for wd in widths],
        compiler_params=pltpu.CompilerParams(dimension_semantics=("arbitrary",),
                                             vmem_limit_bytes=VMEM_LIMIT),
        name="in_proj",
    )(x2, g, w)


def _moba_bias(gate_t, blk):
    nb = gate_t.shape[0]
    blk_id = lax.broadcasted_iota(jnp.int32, gate_t.shape, 0)
    valid = blk_id < blk
    g = jnp.where(valid, gate_t, NEG_INF)
    rank = jnp.zeros(gate_t.shape, jnp.int32)
    for m in range(nb):
        gm = g[m:m + 1, :]
        beats = (gm > g) | ((gm == g) & (m < blk_id))
        rank = rank + beats.astype(jnp.int32)
    sel = valid & (rank < MOBA_TOPK)
    return jnp.where(sel, 0.0, NEG_INF).astype(F32)


def _attn_kernel(q_ref, k_ref, v_ref, o_ref, kb_ref, vta_ref, vtb_ref, kmean_ref, *, seq):
    n_blocks = seq // MOBA_BLOCK
    tq = MOBA_BLOCK
    k = k_ref[...]
    kmean_ref[...] = jnp.mean(k.reshape(n_blocks, MOBA_BLOCK, LANES), axis=1)
    kb_ref[...] = k.astype(BF16)
    vt = v_ref[...].T
    row = lax.broadcasted_iota(jnp.int32, vt.shape, 0)
    vta_ref[...] = jnp.where(row < HEAD_DIM, vt, 0.0).astype(BF16)
    vtb_ref[...] = jnp.where(row >= HEAD_DIM, vt, 0.0).astype(BF16)

    lane = lax.broadcasted_iota(jnp.int32, (tq, LANES), 1)
    k_pos = lax.broadcasted_iota(jnp.int32, (tq, tq), 0)
    q_pos = lax.broadcasted_iota(jnp.int32, (tq, tq), 1)
    causal = k_pos <= q_pos
    nt = (((1,), (1,)), ((), ()))
    scale = HEAD_DIM ** -0.5

    for i in range(n_blocks):
        q = q_ref[i * tq:(i + 1) * tq, :]
        kext = (i + 1) * tq
        acc = None
        for head, vt_ref in ((0, vta_ref), (1, vtb_ref)):
            in_head = (lane < HEAD_DIM) if head == 0 else (lane >= HEAD_DIM)
            qh = jnp.where(in_head, q, 0.0)
            s_t = lax.dot_general(kb_ref[0:kext, :], (qh * scale).astype(BF16), nt,
                                  preferred_element_type=F32)
            pieces = []
            if i > 0:
                gate_t = lax.dot_general(kmean_ref[...], qh, nt, preferred_element_type=F32,
                                         precision=lax.Precision.HIGHEST)
                bias = _moba_bias(gate_t, i)
                for n in range(i):
                    pieces.append(s_t[n * tq:(n + 1) * tq, :] + bias[n:n + 1, :])
            pieces.append(jnp.where(causal, s_t[i * tq:kext, :], NEG_INF))
            s_m = jnp.concatenate(pieces, axis=0) if len(pieces) > 1 else pieces[0]
            m = jnp.max(s_m, axis=0, keepdims=True)
            p = jnp.exp(s_m - m)
            l = jnp.sum(p, axis=0, keepdims=True)
            o_t = jnp.dot(vt_ref[:, 0:kext], p.astype(BF16), preferred_element_type=F32)
            o_t = o_t / l
            acc = o_t if acc is None else acc + o_t
        o_ref[i * tq:(i + 1) * tq, :] = acc.T


def _attention(qkv3):
    b, seq, _ = qkv3.shape
    n_pairs = ATTN_WIDTH // LANES
    kern = functools.partial(_attn_kernel, seq=seq)
    return pl.pallas_call(
        kern,
        grid=(b, n_pairs),
        in_specs=[pl.BlockSpec((None, seq, LANES), lambda bi, j: (bi, 0, j)),
                  pl.BlockSpec((None, seq, LANES), lambda bi, j: (bi, 0, n_pairs + j)),
                  pl.BlockSpec((None, seq, LANES), lambda bi, j: (bi, 0, 2 * n_pairs + j))],
        out_specs=pl.BlockSpec((None, seq, LANES), lambda bi, j: (bi, 0, j)),
        out_shape=jax.ShapeDtypeStruct((b, seq, ATTN_WIDTH), F32),
        scratch_shapes=[pltpu.VMEM((seq, LANES), BF16),
                        pltpu.VMEM((LANES, seq), BF16),
                        pltpu.VMEM((LANES, seq), BF16),
                        pltpu.VMEM((seq // MOBA_BLOCK, LANES), F32)],
        compiler_params=pltpu.CompilerParams(dimension_semantics=("arbitrary", "arbitrary"),
                                             vmem_limit_bytes=VMEM_LIMIT),
        name="moba_attn",
    )(qkv3, qkv3, qkv3)


def _expand_heads(cols):
    rows = cols.shape[0]
    lane = lax.broadcasted_iota(jnp.int32, (rows, LANES), 1)
    out = []
    for pair in range(SSD_HEADS // 2):
        lo = jnp.broadcast_to(cols[:, 2 * pair:2 * pair + 1], (rows, LANES))
        hi = jnp.broadcast_to(cols[:, 2 * pair + 1:2 * pair + 2], (rows, LANES))
        out.append(jnp.where(lane < HEAD_DIM, lo, hi))
    return jnp.concatenate(out, axis=1)


def _softplus(x):
    return jnp.maximum(x, 0.0) + jnp.log1p(jnp.exp(-jnp.abs(x)))


def _silu(x):
    return x * jax.nn.sigmoid(x)


def _ssd_kernel(xbc_ref, z_ref, dt_ref, cw_ref, cb_ref, dtb_ref, alog_ref, dskip_ref, nw_ref,
                o_ref, state_ref, tail_ref, *, seq):
    L = SSD_CHUNK
    n_chunks = seq // L
    gw = SSD_WIDTH // SSD_GROUPS
    state_ref[...] = jnp.zeros(state_ref.shape, F32)
    tail_ref[...] = jnp.zeros(tail_ref.shape, F32)

    r_i = lax.broadcasted_iota(jnp.int32, (L, L), 0)
    c_i = lax.broadcasted_iota(jnp.int32, (L, L), 1)
    tril = r_i >= c_i
    tril_f = tril.astype(F32)
    lane_w = lax.broadcasted_iota(jnp.int32, (L, SSD_WIDTH), 1)
    a_neg = -jnp.exp(alog_ref[...])
    cw = cw_ref[...]

    def chunk(c, carry):
        r0 = pl.multiple_of(c * L, L)
        cur = xbc_ref[pl.ds(r0, L), :]
        xw = jnp.concatenate([tail_ref[...], cur], axis=0)
        tail_ref[...] = cur[L - SUBLANES:, :]
        conv = cb_ref[...] + cw[SSD_CONV - 1:SSD_CONV, :] * cur
        for j in range(SSD_CONV - 1):
            back = SSD_CONV - 1 - j
            conv = conv + cw[j:j + 1, :] * pltpu.roll(xw, back, 0)[SUBLANES:, :]
        act = _silu(conv)
        xs = act[:, :SSD_WIDTH]
        b_in = act[:, SSD_WIDTH:SSD_WIDTH + SSD_GROUPS * SSD_STATE]
        c_in = act[:, SSD_WIDTH + SSD_GROUPS * SSD_STATE:]

        dt = _softplus(dt_ref[pl.ds(r0, L), :] + dtb_ref[...])
        acum = jnp.dot(tril_f, dt * a_neg, preferred_element_type=F32,
                       precision=lax.Precision.HIGHEST)
        acum_t = acum.T
        acum_e = _expand_heads(acum)
        dt_e = _expand_heads(dt)
        last_e = acum_e[L - 1:L, :]
        xdt = xs * dt_e
        xdt_end = (xdt * jnp.exp(last_e - acum_e))

        state = state_ref[...]
        y = xs * dskip_ref[...]
        new_state = state * jnp.exp(last_e)
        y_off = jnp.zeros((L, SSD_WIDTH), F32)
        for g in range(SSD_GROUPS):
            in_g = (lane_w >= g * gw) & (lane_w < (g + 1) * gw)
            bg = b_in[:, g * SSD_STATE:(g + 1) * SSD_STATE]
            cg = c_in[:, g * SSD_STATE:(g + 1) * SSD_STATE].astype(BF16)
            cb = lax.dot_general(cg, bg.astype(BF16), (((1,), (1,)), ((), ())),
                                 preferred_element_type=F32)
            y_off = y_off + jnp.dot(cg, jnp.where(in_g[:SSD_STATE], state, 0.0).astype(BF16),
                                    preferred_element_type=F32)
            new_state = new_state + jnp.dot(bg.T.astype(BF16),
                                            jnp.where(in_g, xdt_end, 0.0).astype(BF16),
                                            preferred_element_type=F32)
            for hh in range(SSD_HEADS // SSD_GROUPS):
                h = g * (SSD_HEADS // SSD_GROUPS) + hh
                diff = jnp.broadcast_to(acum[:, h:h + 1], (L, L)) - acum_t[h:h + 1, :]
                decay = jnp.exp(jnp.where(tril, diff, NEG_INF))
                in_h = (lane_w >= h * HEAD_DIM) & (lane_w < (h + 1) * HEAD_DIM)
                y = y + jnp.dot((cb * decay).astype(BF16), jnp.where(in_h, xdt, 0.0).astype(BF16),
                                preferred_element_type=F32)
        y = y + y_off * jnp.exp(acum_e)
        state_ref[...] = new_state

        y = y * _silu(z_ref[pl.ds(r0, L), :])
        y2 = y * y
        inv = jnp.zeros((L, SSD_WIDTH), F32)
        for g in range(SSD_GROUPS):
            in_g = (lane_w >= g * gw) & (lane_w < (g + 1) * gw)
            ms = jnp.sum(jnp.where(in_g, y2, 0.0), axis=-1, keepdims=True) * (1.0 / gw)
            inv = jnp.where(in_g, lax.rsqrt(ms + NORM_EPS), inv)
        o_ref[pl.ds(r0, L), :] = y * inv * nw_ref[...]
        return carry

    lax.fori_loop(0, n_chunks, chunk, 0)


def _ssd(xbc3, z3, dt3, cw, cb, dtb, alog, dskip, nw):
    b, seq, _ = xbc3.shape
    kern = functools.partial(_ssd_kernel, seq=seq)
    per_batch = lambda wd: pl.BlockSpec((None, seq, wd), lambda bi: (bi, 0, 0))
    return pl.pallas_call(
        kern,
        grid=(b,),
        in_specs=[per_batch(XBC_WIDTH), per_batch(SSD_WIDTH), per_batch(DT_PAD),
                  _resident((SSD_CONV, XBC_WIDTH)), _resident((1, XBC_WIDTH)),
                  _resident((1, DT_PAD)), _resident((1, DT_PAD)),
                  _resident((1, SSD_WIDTH)), _resident((1, SSD_WIDTH))],
        out_specs=per_batch(SSD_WIDTH),
        out_shape=jax.ShapeDtypeStruct((b, seq, SSD_WIDTH), F32),
        scratch_shapes=[pltpu.VMEM((SSD_STATE, SSD_WIDTH), F32),
                        pltpu.VMEM((SUBLANES, XBC_WIDTH), F32)],
        compiler_params=pltpu.CompilerParams(dimension_semantics=("arbitrary",),
                                             vmem_limit_bytes=VMEM_LIMIT),
        name="ssd_mixer",
    )(xbc3, z3, dt3, cw, cb, dtb, alog, dskip, nw)


def _pool_kernel(p_ref, w_ref, s_ref, o_ref, *, seq):
    p = p_ref[...]
    row = lax.broadcasted_iota(jnp.int32, p.shape, 0)
    lane = lax.broadcasted_iota(jnp.int32, p.shape, 1)

    def shifted(x, k):
        return jnp.where(row >= k, pltpu.roll(x, k, 0), 0.0)

    win = p
    total = jnp.zeros(p.shape, F32)
    cnt = jnp.zeros(p.shape, F32)
    width = 1
    for g, w in enumerate(POOL_WINDOWS):
        while width < w:
            win = win + shifted(win, width)
            width *= 2
        in_g = (lane >= g * HEAD_DIM) & (lane < (g + 1) * HEAD_DIM)
        total = jnp.where(in_g, win, total)
        cnt = jnp.where(in_g, jnp.minimum(row + 1, w).astype(F32), cnt)
    d = total / cnt - p
    mixed = jnp.dot(d.astype(BF16), w_ref[...], preferred_element_type=F32)
    o_ref[...] = mixed * s_ref[...]


def _pool(p3, w_bd, scale):
    b, seq, _ = p3.shape
    kern = functools.partial(_pool_kernel, seq=seq)
    return pl.pallas_call(
        kern,
        grid=(b,),
        in_specs=[pl.BlockSpec((None, seq, POOL_WIDTH), lambda bi: (bi, 0, 0)),
                  _resident((POOL_WIDTH, POOL_WIDTH)), _resident((1, POOL_WIDTH))],
        out_specs=pl.BlockSpec((None, seq, POOL_WIDTH), lambda bi: (bi, 0, 0)),
        out_shape=jax.ShapeDtypeStruct((b, seq, POOL_WIDTH), F32),
        compiler_params=pltpu.CompilerParams(dimension_semantics=("arbitrary",),
                                             vmem_limit_bytes=VMEM_LIMIT),
        name="pool_mixer",
    )(p3, w_bd, scale)


def _out_ffn_kernel(x_ref, ya_ref, ys_ref, yp_ref, wo_ref, g_ref, wgu_ref, wd_ref, gf_ref, o_ref,
                    *, final):
    x = x_ref[...]
    x = x + jnp.dot(ya_ref[...].astype(BF16), wo_ref[0:ATTN_WIDTH, :], preferred_element_type=F32)
    x = x + jnp.dot(ys_ref[...].astype(BF16), wo_ref[ATTN_WIDTH:ATTN_WIDTH + SSD_WIDTH, :],
                    preferred_element_type=F32)
    x = x + jnp.dot(yp_ref[...].astype(BF16), wo_ref[ATTN_WIDTH + SSD_WIDTH:, :],
                    preferred_element_type=F32)
    h = _rmsnorm(x, g_ref[...]).astype(BF16)
    gate = jnp.dot(h, wgu_ref[:, :FFN_HIDDEN], preferred_element_type=F32)
    up = jnp.dot(h, wgu_ref[:, FFN_HIDDEN:], preferred_element_type=F32)
    a = (_silu(gate) * up).astype(BF16)
    acc = x + jnp.dot(a, wd_ref[...], preferred_element_type=F32)
    if final:
        acc = _rmsnorm(acc, gf_ref[...])
    o_ref[...] = acc


def _out_ffn(x2, ya, ys, yp, wo, g, wgu, wd, gf, final):
    m = x2.shape[0]
    kern = functools.partial(_out_ffn_kernel, final=final)
    rows = lambda wd_: pl.BlockSpec((ROW_TILE, wd_), lambda i: (i, 0))
    return pl.pallas_call(
        kern,
        grid=(m // ROW_TILE,),
        in_specs=[rows(D_MODEL), rows(ATTN_WIDTH), rows(SSD_WIDTH), rows(POOL_WIDTH),
                  _resident((MIX_WIDTH, D_MODEL)), _resident((1, D_MODEL)),
                  _resident((D_MODEL, 2 * FFN_HIDDEN)), _resident((FFN_HIDDEN, D_MODEL)),
                  _resident((1, D_MODEL))],
        out_specs=rows(D_MODEL),
        out_shape=jax.ShapeDtypeStruct((m, D_MODEL), F32),
        compiler_params=pltpu.CompilerParams(dimension_semantics=("arbitrary",),
                                             vmem_limit_bytes=VMEM_LIMIT),
        name="out_ffn",
    )(x2, ya, ys, yp, wo, g, wgu, wd, gf)


def _pad_lanes(v, width):
    return jnp.pad(v, [(0, 0)] * (v.ndim - 1) + [(0, width - v.shape[-1])])


def kernel(x, norm_mix, w_in, conv_w, conv_b, dt_bias, a_log, d_skip, ssd_norm, pool_w, pool_scale,
           w_out, norm_ffn, w_gate_up, w_down, norm_final):
    bsz, seq, d = x.shape
    depth = w_in.shape[0]
    m = bsz * seq
    dt_lo = 3 * ATTN_WIDTH + SSD_WIDTH + XBC_WIDTH
    w_in_p = jnp.concatenate(
        [w_in[:, :, :dt_lo], _pad_lanes(w_in[:, :, dt_lo:dt_lo + SSD_HEADS], DT_PAD),
         w_in[:, :, dt_lo + SSD_HEADS:]], axis=-1).astype(BF16)
    w_out_b = w_out.astype(BF16)
    w_gu_b = w_gate_up.astype(BF16)
    w_d_b = w_down.astype(BF16)
    eye = jnp.eye(len(POOL_WINDOWS), dtype=F32)
    pool_bd = (eye[None, :, None, :, None] * pool_w[:, :, :, None, :]).reshape(
        depth, POOL_WIDTH, POOL_WIDTH).astype(BF16)
    dtb_p = _pad_lanes(dt_bias, DT_PAD)
    alog_p = _pad_lanes(a_log, DT_PAD)
    dskip_e = jnp.repeat(d_skip, HEAD_DIM, axis=-1)

    x2 = x.reshape(m, d)
    gf = norm_final.reshape(1, d)
    for l in range(depth):
        qkv, z, xbc, dt, p = _in_proj(x2, norm_mix[l].reshape(1, d), w_in_p[l])
        ya = _attention(qkv.reshape(bsz, seq, QKV_W))
        ys = _ssd(xbc.reshape(bsz, seq, XBC_WIDTH), z.reshape(bsz, seq, SSD_WIDTH),
                  dt.reshape(bsz, seq, DT_PAD), conv_w[l], conv_b[l].reshape(1, -1),
                  dtb_p[l].reshape(1, -1), alog_p[l].reshape(1, -1), dskip_e[l].reshape(1, -1),
                  ssd_norm[l].reshape(1, -1))
        yp = _pool(p.reshape(bsz, seq, POOL_WIDTH), pool_bd[l], pool_scale[l].reshape(1, -1))
        x2 = _out_ffn(x2, ya.reshape(m, ATTN_WIDTH), ys.reshape(m, SSD_WIDTH),
                      yp.reshape(m, POOL_WIDTH), w_out_b[l], norm_ffn[l].reshape(1, d),
                      w_gu_b[l], w_d_b[l], gf, final=(l == depth - 1))
    return x2.reshape(bsz, seq, d)
```

```python
import functools
import math

import jax
import jax.numpy as jnp
from jax import lax
from jax.experimental import pallas as pl
from jax.experimental.pallas import tpu as pltpu

F32 = jnp.float32
BF16 = jnp.bfloat16

D_MODEL = 1024
ATTN_HEADS = 6
HEAD_DIM = 64
ATTN_WIDTH = ATTN_HEADS * HEAD_DIM
MOBA_BLOCK = 256
MOBA_TOPK = 3
SSD_HEADS = 6
SSD_WIDTH = SSD_HEADS * HEAD_DIM
SSD_GROUPS = 2
SSD_STATE = 128
SSD_CONV = 4
SSD_CHUNK = 128
XBC_WIDTH = SSD_WIDTH + 2 * SSD_GROUPS * SSD_STATE
POOL_WINDOWS = (2, 4, 8, 16)
POOL_WIDTH = 256
MIX_WIDTH = 1024
FFN_HIDDEN = 2816
NORM_EPS = 1e-6
NEG_INF = -1e30

LANES = 128
SUBLANES = 8
QKV_W = 3 * ATTN_WIDTH
DT_PAD = LANES
COL_Z = QKV_W
COL_XBC = COL_Z + SSD_WIDTH
COL_DT = COL_XBC + XBC_WIDTH
COL_P = COL_DT + DT_PAD
IN_PROJ_PAD = COL_P + POOL_WIDTH

ROW_TILE = 512
SSD_UNROLL = 2
QK_LOOKAHEAD = 2
VMEM_LIMIT = 56 * 1024 * 1024


def _resident(shape):
    nd = len(shape)
    return pl.BlockSpec(shape, lambda *_: (0,) * nd, pipeline_mode=pl.Buffered(1))


def _rmsnorm(x, g):
    ms = jnp.mean(x * x, axis=-1, keepdims=True)
    return x * lax.rsqrt(ms + NORM_EPS) * g


def _softplus(x):
    return jnp.maximum(x, 0.0) + jnp.log1p(jnp.exp(-jnp.abs(x)))


def _silu(x):
    return x * jax.nn.sigmoid(x)


def _rows_back(x, tail_ref, k):
    n_tail = tail_ref.shape[0]
    xw = jnp.concatenate([tail_ref[...], x], axis=0)
    return pltpu.roll(xw, k, 0)[n_tail:, :]


def _in_proj_kernel(x_ref, g_ref, w_ref, cw_ref, cb_ref, dtb_ref, pw_ref, ps_ref,
                    qkv_ref, zs_ref, xc_ref, dt_ref, yp_ref, ctail_ref, ptail_ref,
                    *, tiles_per_seq):
    seq_tile = lax.rem(pl.program_id(0), tiles_per_seq)

    @pl.when(seq_tile == 0)
    def _():
        ctail_ref[...] = jnp.zeros(ctail_ref.shape, F32)
        ptail_ref[...] = jnp.zeros(ptail_ref.shape, F32)

    h = _rmsnorm(x_ref[...], g_ref[...]).astype(BF16)

    xbc = jnp.dot(h, w_ref[:, COL_XBC:COL_DT], preferred_element_type=F32)
    p = jnp.dot(h, w_ref[:, COL_P:IN_PROJ_PAD], preferred_element_type=F32)
    z = jnp.dot(h, w_ref[:, COL_Z:COL_XBC], preferred_element_type=F32)
    dt_raw = jnp.dot(h, w_ref[:, COL_DT:COL_P], preferred_element_type=F32)
    qkv_ref[...] = jnp.dot(h, w_ref[:, 0:COL_Z], preferred_element_type=F32)

    cw = cw_ref[...]
    conv = cb_ref[...] + cw[SSD_CONV - 1:SSD_CONV, :] * xbc
    for j in range(SSD_CONV - 1):
        conv = conv + cw[j:j + 1, :] * _rows_back(xbc, ctail_ref, SSD_CONV - 1 - j)
    ctail_ref[...] = xbc[ROW_TILE - ctail_ref.shape[0]:, :]
    xc_ref[...] = _silu(conv)
    zs_ref[...] = _silu(z)
    dt_ref[...] = _softplus(dt_raw + dtb_ref[...])

    n_tail = ptail_ref.shape[0]
    win = jnp.concatenate([ptail_ref[...], p], axis=0)
    ptail_ref[...] = p[ROW_TILE - n_tail:, :]
    pos = seq_tile * ROW_TILE + lax.broadcasted_iota(jnp.int32, p.shape, 0)
    lane = lax.broadcasted_iota(jnp.int32, p.shape, 1)
    total = jnp.zeros(p.shape, F32)
    cnt = jnp.ones(p.shape, F32)
    width = 1
    for gi, w in enumerate(POOL_WINDOWS):
        while width < w:
            win = win + pltpu.roll(win, width, 0)
            width *= 2
        in_g = (lane >= gi * HEAD_DIM) & (lane < (gi + 1) * HEAD_DIM)
        total = jnp.where(in_g, win[n_tail:, :], total)
        cnt = jnp.where(in_g, jnp.minimum(pos + 1, w).astype(F32), cnt)
    d = total / cnt - p
    yp_ref[...] = jnp.dot(d.astype(BF16), pw_ref[...], preferred_element_type=F32) * ps_ref[...]


def _in_proj(x2, g, w, cw, cb, dtb, pw, ps, seq):
    m = x2.shape[0]
    outs = ((QKV_W, F32), (SSD_WIDTH, F32), (XBC_WIDTH, F32), (DT_PAD, F32), (POOL_WIDTH, F32))
    kern = functools.partial(_in_proj_kernel, tiles_per_seq=seq // ROW_TILE)
    return pl.pallas_call(
        kern,
        grid=(m // ROW_TILE,),
        in_specs=[pl.BlockSpec((ROW_TILE, D_MODEL), lambda i: (i, 0)),
                  _resident((1, D_MODEL)),
                  _resident((D_MODEL, IN_PROJ_PAD)),
                  _resident((SSD_CONV, XBC_WIDTH)), _resident((1, XBC_WIDTH)),
                  _resident((1, DT_PAD)),
                  _resident((POOL_WIDTH, POOL_WIDTH)), _resident((1, POOL_WIDTH))],
        out_specs=[pl.BlockSpec((ROW_TILE, wd), lambda i: (i, 0)) for wd, _ in outs],
        out_shape=[jax.ShapeDtypeStruct((m, wd), dt) for wd, dt in outs],
        scratch_shapes=[pltpu.VMEM((SUBLANES, XBC_WIDTH), F32),
                        pltpu.VMEM((2 * SUBLANES, POOL_WIDTH), F32)],
        compiler_params=pltpu.CompilerParams(dimension_semantics=("arbitrary",),
                                             vmem_limit_bytes=VMEM_LIMIT),
        name="in_proj",
    )(x2, g, w, cw, cb, dtb, pw, ps)


def _moba_bias(gate_t, blk):
    nb = gate_t.shape[0]
    blk_id = lax.broadcasted_iota(jnp.int32, gate_t.shape, 0)
    valid = blk_id < blk
    g = jnp.where(valid, gate_t, NEG_INF)
    rank = jnp.zeros(gate_t.shape, jnp.int32)
    for m in range(nb):
        gm = g[m:m + 1, :]
        beats = (gm > g) | ((gm == g) & (m < blk_id))
        rank = rank + beats.astype(jnp.int32)
    sel = valid & (rank < MOBA_TOPK)
    return jnp.where(sel, 0.0, NEG_INF).astype(F32)


def _attn_kernel(q_ref, k_ref, v_ref, o_ref, kb_ref, vta_ref, vtb_ref, kmean_ref, *, seq):
    n_blocks = seq // MOBA_BLOCK
    tq = MOBA_BLOCK
    k = k_ref[...]
    kmean_ref[...] = jnp.mean(k.reshape(n_blocks, MOBA_BLOCK, LANES), axis=1)
    kb_ref[...] = k.astype(BF16)
    vt = v_ref[...].T
    row = lax.broadcasted_iota(jnp.int32, vt.shape, 0)
    vta_ref[...] = jnp.where(row < HEAD_DIM, vt, jnp.where(row == HEAD_DIM, 1.0, 0.0)).astype(BF16)
    vtb_ref[...] = jnp.where(row >= HEAD_DIM, vt, jnp.where(row == 0, 1.0, 0.0)).astype(BF16)

    lane = lax.broadcasted_iota(jnp.int32, (tq, LANES), 1)
    k_pos = lax.broadcasted_iota(jnp.int32, (tq, tq), 0)
    q_pos = lax.broadcasted_iota(jnp.int32, (tq, tq), 1)
    causal = k_pos <= q_pos
    out_row = lax.broadcasted_iota(jnp.int32, (LANES, tq), 0)
    nt = (((1,), (1,)), ((), ()))
    scale2 = HEAD_DIM ** -0.5 * math.log2(math.e)

    def scores(i, head):
        q = q_ref[i * tq:(i + 1) * tq, :]
        in_head = (lane < HEAD_DIM) if head == 0 else (lane >= HEAD_DIM)
        qh = jnp.where(in_head, q, 0.0)
        s_t = lax.dot_general(kb_ref[0:(i + 1) * tq, :], (qh * scale2).astype(BF16), nt,
                              preferred_element_type=F32)
        bias = None
        if i > MOBA_TOPK:
            gate_t = lax.dot_general(kmean_ref[...], qh, nt, preferred_element_type=F32,
                                     precision=lax.Precision.HIGHEST)
            bias = _moba_bias(gate_t, i)
        return s_t, bias

    def attend(i, head, s_t, bias):
        vt_ref = vta_ref if head == 0 else vtb_ref
        kext = (i + 1) * tq
        diag = jnp.where(causal, s_t[i * tq:kext, :], NEG_INF)
        m = jnp.max(diag, axis=0, keepdims=True)
        if bias is not None:
            past = [s_t[n * tq:(n + 1) * tq, :] + bias[n:n + 1, :] for n in range(i)]
            bounds = [(n * tq, (n + 1) * tq) for n in range(i)]
        elif i > 0:
            past = [s_t[0:i * tq, :]]
            bounds = [(0, i * tq)]
        else:
            past, bounds = [], []
        for piece in past:
            m = jnp.maximum(m, jnp.max(piece, axis=0, keepdims=True))
        acc = jnp.dot(vt_ref[:, i * tq:kext], jnp.exp2(diag - m).astype(BF16),
                      preferred_element_type=F32)
        for piece, (lo, hi) in zip(past, bounds):
            acc = acc + jnp.dot(vt_ref[:, lo:hi], jnp.exp2(piece - m).astype(BF16),
                                preferred_element_type=F32)
        return acc

    items = [(i, head) for i in range(n_blocks) for head in (0, 1)]
    pending = [scores(*it) for it in items[:QK_LOOKAHEAD]]
    accs = []
    for j, (i, head) in enumerate(items):
        if j + QK_LOOKAHEAD < len(items):
            pending.append(scores(*items[j + QK_LOOKAHEAD]))
        accs.append(attend(i, head, *pending.pop(0)))
        if head == 1:
            out_t = jnp.where(out_row < HEAD_DIM, accs[0] / accs[0][HEAD_DIM:HEAD_DIM + 1, :],
                              accs[1] / accs[1][0:1, :])
            o_ref[i * tq:(i + 1) * tq, :] = out_t.T
            accs = []


def _attention(qkv3):
    b, seq, _ = qkv3.shape
    n_pairs = ATTN_WIDTH // LANES
    kern = functools.partial(_attn_kernel, seq=seq)
    return pl.pallas_call(
        kern,
        grid=(b, n_pairs),
        in_specs=[pl.BlockSpec((None, seq, LANES), lambda bi, j: (bi, 0, j)),
                  pl.BlockSpec((None, seq, LANES), lambda bi, j: (bi, 0, n_pairs + j)),
                  pl.BlockSpec((None, seq, LANES), lambda bi, j: (bi, 0, 2 * n_pairs + j))],
        out_specs=pl.BlockSpec((None, seq, LANES), lambda bi, j: (bi, 0, j)),
        out_shape=jax.ShapeDtypeStruct((b, seq, ATTN_WIDTH), F32),
        scratch_shapes=[pltpu.VMEM((seq, LANES), BF16),
                        pltpu.VMEM((LANES, seq), BF16),
                        pltpu.VMEM((LANES, seq), BF16),
                        pltpu.VMEM((seq // MOBA_BLOCK, LANES), F32)],
        compiler_params=pltpu.CompilerParams(dimension_semantics=("arbitrary", "arbitrary"),
                                             vmem_limit_bytes=VMEM_LIMIT),
        name="moba_attn",
    )(qkv3, qkv3, qkv3)


def _expand_heads(cols):
    rows = cols.shape[0]
    lane = lax.broadcasted_iota(jnp.int32, (rows, LANES), 1)
    out = []
    for pair in range(SSD_HEADS // 2):
        lo = jnp.broadcast_to(cols[:, 2 * pair:2 * pair + 1], (rows, LANES))
        hi = jnp.broadcast_to(cols[:, 2 * pair + 1:2 * pair + 2], (rows, LANES))
        out.append(jnp.where(lane < HEAD_DIM, lo, hi))
    return jnp.concatenate(out, axis=1)


def _cumsum_rows(x):
    n = x.shape[0]
    row = lax.broadcasted_iota(jnp.int32, x.shape, 0)
    k = 1
    while k < n:
        x = x + jnp.where(row >= k, pltpu.roll(x, k, 0), 0.0)
        k *= 2
    return x


def _ssd_kernel(xc_ref, zs_ref, dt_ref, alog_ref, dskip_ref, nw_ref, o_ref, state_ref, *, seq):
    L = SSD_CHUNK
    n_chunks = seq // L
    gw = SSD_WIDTH // SSD_GROUPS
    state_ref[...] = jnp.zeros(state_ref.shape, F32)

    r_i = lax.broadcasted_iota(jnp.int32, (L, L), 0)
    c_i = lax.broadcasted_iota(jnp.int32, (L, L), 1)
    tril = r_i >= c_i
    lane_w = lax.broadcasted_iota(jnp.int32, (L, SSD_WIDTH), 1)
    a_neg = -jnp.exp(alog_ref[...])

    def chunk(c, carry):
        r0 = pl.multiple_of(c * L, L)
        act = xc_ref[pl.ds(r0, L), :]
        xs = act[:, :SSD_WIDTH]
        b_in = act[:, SSD_WIDTH:SSD_WIDTH + SSD_GROUPS * SSD_STATE]
        c_in = act[:, SSD_WIDTH + SSD_GROUPS * SSD_STATE:]

        dt = dt_ref[pl.ds(r0, L), :]
        acum = _cumsum_rows(dt * a_neg)
        acum_t = acum.T
        acum_e = _expand_heads(acum)
        dt_e = _expand_heads(dt)
        last_e = acum_e[L - 1:L, :]
        xdt = xs * dt_e
        xdt_end = (xdt * jnp.exp(last_e - acum_e))

        state = state_ref[...]
        y = xs * dskip_ref[...]
        new_state = state * jnp.exp(last_e)
        y_off = jnp.zeros((L, SSD_WIDTH), F32)
        for g in range(SSD_GROUPS):
            in_g = (lane_w >= g * gw) & (lane_w < (g + 1) * gw)
            bg = b_in[:, g * SSD_STATE:(g + 1) * SSD_STATE]
            cg = c_in[:, g * SSD_STATE:(g + 1) * SSD_STATE].astype(BF16)
            cb = lax.dot_general(cg, bg.astype(BF16), (((1,), (1,)), ((), ())),
                                 preferred_element_type=F32)
            y_off = y_off + jnp.dot(cg, jnp.where(in_g[:SSD_STATE], state, 0.0).astype(BF16),
                                    preferred_element_type=F32)
            new_state = new_state + jnp.dot(bg.T.astype(BF16),
                                            jnp.where(in_g, xdt_end, 0.0).astype(BF16),
                                            preferred_element_type=F32)
            for hh in range(SSD_HEADS // SSD_GROUPS):
                h = g * (SSD_HEADS // SSD_GROUPS) + hh
                diff = jnp.broadcast_to(acum[:, h:h + 1], (L, L)) - acum_t[h:h + 1, :]
                decay = jnp.exp(jnp.where(tril, diff, NEG_INF))
                in_h = (lane_w >= h * HEAD_DIM) & (lane_w < (h + 1) * HEAD_DIM)
                y = y + jnp.dot((cb * decay).astype(BF16), jnp.where(in_h, xdt, 0.0).astype(BF16),
                                preferred_element_type=F32)
        y = y + y_off * jnp.exp(acum_e)
        state_ref[...] = new_state

        y = y * zs_ref[pl.ds(r0, L), :]
        y2 = y * y
        inv = jnp.zeros((L, SSD_WIDTH), F32)
        for g in range(SSD_GROUPS):
            in_g = (lane_w >= g * gw) & (lane_w < (g + 1) * gw)
            ms = jnp.sum(jnp.where(in_g, y2, 0.0), axis=-1, keepdims=True) * (1.0 / gw)
            inv = jnp.where(in_g, lax.rsqrt(ms + NORM_EPS), inv)
        o_ref[pl.ds(r0, L), :] = y * inv * nw_ref[...]
        return carry

    lax.fori_loop(0, n_chunks, chunk, 0, unroll=SSD_UNROLL)


def _ssd(xc3, zs3, dt3, alog, dskip, nw):
    b, seq, _ = xc3.shape
    kern = functools.partial(_ssd_kernel, seq=seq)
    per_batch = lambda wd: pl.BlockSpec((None, seq, wd), lambda bi: (bi, 0, 0))
    return pl.pallas_call(
        kern,
        grid=(b,),
        in_specs=[per_batch(XBC_WIDTH), per_batch(SSD_WIDTH), per_batch(DT_PAD),
                  _resident((1, DT_PAD)), _resident((1, SSD_WIDTH)), _resident((1, SSD_WIDTH))],
        out_specs=per_batch(SSD_WIDTH),
        out_shape=jax.ShapeDtypeStruct((b, seq, SSD_WIDTH), F32),
        scratch_shapes=[pltpu.VMEM((SSD_STATE, SSD_WIDTH), F32)],
        compiler_params=pltpu.CompilerParams(dimension_semantics=("arbitrary",),
                                             vmem_limit_bytes=VMEM_LIMIT),
        name="ssd_mixer",
    )(xc3, zs3, dt3, alog, dskip, nw)


def _out_ffn_kernel(x_ref, ya_ref, ys_ref, yp_ref, wo_ref, g_ref, wgu_ref, wd_ref, gf_ref, o_ref,
                    *, final):
    x = x_ref[...]
    x = x + jnp.dot(ya_ref[...].astype(BF16), wo_ref[0:ATTN_WIDTH, :], preferred_element_type=F32)
    x = x + jnp.dot(ys_ref[...].astype(BF16), wo_ref[ATTN_WIDTH:ATTN_WIDTH + SSD_WIDTH, :],
                    preferred_element_type=F32)
    x = x + jnp.dot(yp_ref[...].astype(BF16), wo_ref[ATTN_WIDTH + SSD_WIDTH:, :],
                    preferred_element_type=F32)
    h = _rmsnorm(x, g_ref[...]).astype(BF16)
    gate = jnp.dot(h, wgu_ref[:, :FFN_HIDDEN], preferred_element_type=F32)
    up = jnp.dot(h, wgu_ref[:, FFN_HIDDEN:], preferred_element_type=F32)
    a = (_silu(gate) * up).astype(BF16)
    acc = x + jnp.dot(a, wd_ref[...], preferred_element_type=F32)
    if final:
        acc = _rmsnorm(acc, gf_ref[...])
    o_ref[...] = acc


def _out_ffn(x2, ya, ys, yp, wo, g, wgu, wd, gf, final):
    m = x2.shape[0]
    kern = functools.partial(_out_ffn_kernel, final=final)
    rows = lambda wd_: pl.BlockSpec((ROW_TILE, wd_), lambda i: (i, 0))
    return pl.pallas_call(
        kern,
        grid=(m // ROW_TILE,),
        in_specs=[rows(D_MODEL), rows(ATTN_WIDTH), rows(SSD_WIDTH), rows(POOL_WIDTH),
                  _resident((MIX_WIDTH, D_MODEL)), _resident((1, D_MODEL)),
                  _resident((D_MODEL, 2 * FFN_HIDDEN)), _resident((FFN_HIDDEN, D_MODEL)),
                  _resident((1, D_MODEL))],
        out_specs=rows(D_MODEL),
        out_shape=jax.ShapeDtypeStruct((m, D_MODEL), F32),
        compiler_params=pltpu.CompilerParams(dimension_semantics=("arbitrary",),
                                             vmem_limit_bytes=VMEM_LIMIT),
        name="out_ffn",
    )(x2, ya, ys, yp, wo, g, wgu, wd, gf)


def _pad_lanes(v, width):
    return jnp.pad(v, [(0, 0)] * (v.ndim - 1) + [(0, width - v.shape[-1])])


def kernel(x, norm_mix, w_in, conv_w, conv_b, dt_bias, a_log, d_skip, ssd_norm, pool_w, pool_scale,
           w_out, norm_ffn, w_gate_up, w_down, norm_final):
    bsz, seq, d = x.shape
    depth = w_in.shape[0]
    m = bsz * seq
    dt_lo = 3 * ATTN_WIDTH + SSD_WIDTH + XBC_WIDTH
    w_in_p = jnp.concatenate(
        [w_in[:, :, :dt_lo], _pad_lanes(w_in[:, :, dt_lo:dt_lo + SSD_HEADS], DT_PAD),
         w_in[:, :, dt_lo + SSD_HEADS:]], axis=-1).astype(BF16)
    w_out_b = w_out.astype(BF16)
    w_gu_b = w_gate_up.astype(BF16)
    w_d_b = w_down.astype(BF16)
    eye = jnp.eye(len(POOL_WINDOWS), dtype=F32)
    pool_bd = (eye[None, :, None, :, None] * pool_w[:, :, :, None, :]).reshape(
        depth, POOL_WIDTH, POOL_WIDTH).astype(BF16)
    dtb_p = _pad_lanes(dt_bias, DT_PAD)
    alog_p = _pad_lanes(a_log, DT_PAD)
    dskip_e = jnp.repeat(d_skip, HEAD_DIM, axis=-1)

    x2 = x.reshape(m, d)
    gf = norm_final.reshape(1, d)
    for l in range(depth):
        qkv, zs, xc, dt, yp = _in_proj(x2, norm_mix[l].reshape(1, d), w_in_p[l], conv_w[l],
                                       conv_b[l].reshape(1, -1), dtb_p[l].reshape(1, -1),
                                       pool_bd[l], pool_scale[l].reshape(1, -1), seq)
        ya = _attention(qkv.reshape(bsz, seq, QKV_W))
        ys = _ssd(xc.reshape(bsz, seq, XBC_WIDTH), zs.reshape(bsz, seq, SSD_WIDTH),
                  dt.reshape(bsz, seq, DT_PAD), alog_p[l].reshape(1, -1),
                  dskip_e[l].reshape(1, -1), ssd_norm[l].reshape(1, -1))
        x2 = _out_ffn(x2, ya.reshape(m, ATTN_WIDTH), ys.reshape(m, SSD_WIDTH), yp,
                      w_out_b[l], norm_ffn[l].reshape(1, d), w_gu_b[l], w_d_b[l], gf,
                      final=(l == depth - 1))
    return x2.reshape(bsz, seq, d)
```

```python
import functools
import math

import jax
import jax.numpy as jnp
from jax import lax
from jax.experimental import pallas as pl
from jax.experimental.pallas import tpu as pltpu

F32 = jnp.float32
BF16 = jnp.bfloat16

D_MODEL = 1024
ATTN_HEADS = 6
HEAD_DIM = 64
ATTN_WIDTH = ATTN_HEADS * HEAD_DIM
MOBA_BLOCK = 256
MOBA_TOPK = 3
SSD_HEADS = 6
SSD_WIDTH = SSD_HEADS * HEAD_DIM
SSD_GROUPS = 2
SSD_STATE = 128
SSD_CONV = 4
SSD_CHUNK = 128
XBC_WIDTH = SSD_WIDTH + 2 * SSD_GROUPS * SSD_STATE
POOL_WINDOWS = (2, 4, 8, 16)
POOL_WIDTH = 256
MIX_WIDTH = 1024
FFN_HIDDEN = 2816
NORM_EPS = 1e-6
NEG_INF = -1e30

LANES = 128
SUBLANES = 8
QKV_W = 3 * ATTN_WIDTH
DT_PAD = LANES
COL_Z = QKV_W
COL_XBC = COL_Z + SSD_WIDTH
COL_DT = COL_XBC + XBC_WIDTH
COL_P = COL_DT + DT_PAD
IN_PROJ_PAD = COL_P + POOL_WIDTH

ROW_TILE = 512
SSD_UNROLL = 4
QK_LOOKAHEAD = 3
VMEM_LIMIT = 56 * 1024 * 1024


def _resident(shape, layer=None):
    nd = len(shape)
    if layer is None:
        return pl.BlockSpec(shape, lambda *_: (0,) * nd, pipeline_mode=pl.Buffered(1))
    return pl.BlockSpec((None,) + tuple(shape), lambda *_: (layer,) + (0,) * nd,
                        pipeline_mode=pl.Buffered(1))


def _rmsnorm(x, g):
    ms = jnp.mean(x * x, axis=-1, keepdims=True)
    return x * lax.rsqrt(ms + NORM_EPS) * g


def _softplus(x):
    return jnp.maximum(x, 0.0) + jnp.log1p(jnp.exp(-jnp.abs(x)))


def _silu(x):
    return x * jax.nn.sigmoid(x)


def _rows_back(x, tail_ref, k):
    n_tail = tail_ref.shape[0]
    xw = jnp.concatenate([tail_ref[...], x], axis=0)
    return pltpu.roll(xw, k, 0)[n_tail:, :]


def _in_proj_kernel(x_ref, g_ref, w_ref, cw_ref, cb_ref, dtb_ref, pw_ref, ps_ref,
                    wo_f_ref, wgu_f_ref, wd_f_ref,
                    qkv_ref, zs_ref, xc_ref, dt_ref, yp_ref, wo_b_ref, wgu_b_ref, wd_b_ref,
                    ctail_ref, ptail_ref, *, tiles_per_seq):
    wo_b_ref[...] = wo_f_ref[...].astype(BF16)
    wgu_b_ref[...] = wgu_f_ref[...].astype(BF16)
    wd_b_ref[...] = wd_f_ref[...].astype(BF16)

    seq_tile = lax.rem(pl.program_id(0), tiles_per_seq)

    @pl.when(seq_tile == 0)
    def _():
        ctail_ref[...] = jnp.zeros(ctail_ref.shape, F32)
        ptail_ref[...] = jnp.zeros(ptail_ref.shape, F32)

    h = _rmsnorm(x_ref[...], g_ref[...]).astype(BF16)

    xbc = jnp.dot(h, w_ref[:, COL_XBC:COL_DT], preferred_element_type=F32)
    p = jnp.dot(h, w_ref[:, COL_P:IN_PROJ_PAD], preferred_element_type=F32)
    qkv_ref[...] = jnp.dot(h, w_ref[:, 0:COL_Z], preferred_element_type=F32)
    z = jnp.dot(h, w_ref[:, COL_Z:COL_XBC], preferred_element_type=F32)
    dt_raw = jnp.dot(h, w_ref[:, COL_DT:COL_P], preferred_element_type=F32)

    cw = cw_ref[...]
    conv = cb_ref[...] + cw[SSD_CONV - 1:SSD_CONV, :] * xbc
    for j in range(SSD_CONV - 1):
        conv = conv + cw[j:j + 1, :] * _rows_back(xbc, ctail_ref, SSD_CONV - 1 - j)
    ctail_ref[...] = xbc[ROW_TILE - ctail_ref.shape[0]:, :]
    xc_ref[...] = _silu(conv)

    n_tail = ptail_ref.shape[0]
    win = jnp.concatenate([ptail_ref[...], p], axis=0)
    ptail_ref[...] = p[ROW_TILE - n_tail:, :]
    pos = seq_tile * ROW_TILE + lax.broadcasted_iota(jnp.int32, p.shape, 0)
    lane = lax.broadcasted_iota(jnp.int32, p.shape, 1)
    total = jnp.zeros(p.shape, F32)
    cnt = jnp.ones(p.shape, F32)
    width = 1
    for gi, w in enumerate(POOL_WINDOWS):
        while width < w:
            win = win + pltpu.roll(win, width, 0)
            width *= 2
        in_g = (lane >= gi * HEAD_DIM) & (lane < (gi + 1) * HEAD_DIM)
        total = jnp.where(in_g, win[n_tail:, :], total)
        cnt = jnp.where(in_g, jnp.minimum(pos + 1, w).astype(F32), cnt)
    d = total / cnt - p
    zs_ref[...] = _silu(z)
    dt_ref[...] = _softplus(dt_raw + dtb_ref[...])
    yp_ref[...] = jnp.dot(d.astype(BF16), pw_ref[...], preferred_element_type=F32) * ps_ref[...]


def _slab_specs(rows, cols, steps, layer):
    bf16_rows = 2 * SUBLANES
    n_slabs = max(n for n in range(1, steps + 1)
                  if steps % n == 0 and rows % (n * bf16_rows) == 0)
    slab = rows // n_slabs
    per = steps // n_slabs
    src = pl.BlockSpec((None, slab, cols), lambda i: (layer, i // per, 0))
    dst = pl.BlockSpec((slab, cols), lambda i: (i // per, 0))
    return src, dst


def _in_proj(x2, g, w, cw, cb, dtb, pw, ps, wo, wgu, wd, seq, layer):
    m = x2.shape[0]
    steps = m // ROW_TILE
    outs = ((QKV_W, F32), (SSD_WIDTH, F32), (XBC_WIDTH, F32), (DT_PAD, F32), (POOL_WIDTH, F32))
    slabs = [_slab_specs(a.shape[1], a.shape[2], steps, layer) for a in (wo, wgu, wd)]
    kern = functools.partial(_in_proj_kernel, tiles_per_seq=seq // ROW_TILE)
    return pl.pallas_call(
        kern,
        grid=(steps,),
        in_specs=[pl.BlockSpec((ROW_TILE, D_MODEL), lambda i: (i, 0)),
                  _resident((1, D_MODEL), layer),
                  _resident((D_MODEL, IN_PROJ_PAD), layer),
                  _resident((SSD_CONV, XBC_WIDTH), layer), _resident((1, XBC_WIDTH), layer),
                  _resident((1, DT_PAD), layer),
                  _resident((POOL_WIDTH, POOL_WIDTH), layer), _resident((1, POOL_WIDTH), layer)]
        + [src for src, _ in slabs],
        out_specs=[pl.BlockSpec((ROW_TILE, wd_), lambda i: (i, 0)) for wd_, _ in outs]
        + [dst for _, dst in slabs],
        out_shape=[jax.ShapeDtypeStruct((m, wd_), dt) for wd_, dt in outs]
        + [jax.ShapeDtypeStruct(a.shape[1:], BF16) for a in (wo, wgu, wd)],
        scratch_shapes=[pltpu.VMEM((SUBLANES, XBC_WIDTH), F32),
                        pltpu.VMEM((2 * SUBLANES, POOL_WIDTH), F32)],
        compiler_params=pltpu.CompilerParams(dimension_semantics=("arbitrary",),
                                             vmem_limit_bytes=VMEM_LIMIT),
        name="in_proj",
    )(x2, g, w, cw, cb, dtb, pw, ps, wo, wgu, wd)


def _moba_bias(gate_t, blk):
    nb = gate_t.shape[0]
    blk_id = lax.broadcasted_iota(jnp.int32, gate_t.shape, 0)
    valid = blk_id < blk
    g = jnp.where(valid, gate_t, NEG_INF)
    rank = jnp.zeros(gate_t.shape, jnp.int32)
    for m in range(nb):
        gm = g[m:m + 1, :]
        beats = (gm > g) | ((gm == g) & (m < blk_id))
        rank = rank + beats.astype(jnp.int32)
    sel = valid & (rank < MOBA_TOPK)
    return jnp.where(sel, 0.0, NEG_INF).astype(F32)


def _attn_kernel(q_ref, k_ref, v_ref, o_ref, kb_ref, vta_ref, vtb_ref, kmean_ref, *, seq):
    n_blocks = seq // MOBA_BLOCK
    tq = MOBA_BLOCK
    k = k_ref[...]
    kmean_ref[...] = jnp.mean(k.reshape(n_blocks, MOBA_BLOCK, LANES), axis=1)
    kb_ref[...] = k.astype(BF16)
    vt = v_ref[...].T
    row = lax.broadcasted_iota(jnp.int32, vt.shape, 0)
    vta_ref[...] = jnp.where(row < HEAD_DIM, vt, jnp.where(row == HEAD_DIM, 1.0, 0.0)).astype(BF16)
    vtb_ref[...] = jnp.where(row >= HEAD_DIM, vt, jnp.where(row == 0, 1.0, 0.0)).astype(BF16)

    lane = lax.broadcasted_iota(jnp.int32, (tq, LANES), 1)
    k_pos = lax.broadcasted_iota(jnp.int32, (tq, tq), 0)
    q_pos = lax.broadcasted_iota(jnp.int32, (tq, tq), 1)
    causal = k_pos <= q_pos
    out_row = lax.broadcasted_iota(jnp.int32, (LANES, tq), 0)
    nt = (((1,), (1,)), ((), ()))
    scale2 = HEAD_DIM ** -0.5 * math.log2(math.e)

    def scores(i, head):
        q = q_ref[i * tq:(i + 1) * tq, :]
        in_head = (lane < HEAD_DIM) if head == 0 else (lane >= HEAD_DIM)
        qh = jnp.where(in_head, q, 0.0)
        s_t = lax.dot_general(kb_ref[0:(i + 1) * tq, :], (qh * scale2).astype(BF16), nt,
                              preferred_element_type=F32)
        bias = None
        if i > MOBA_TOPK:
            gate_t = lax.dot_general(kmean_ref[...], qh, nt, preferred_element_type=F32,
                                     precision=lax.Precision.HIGHEST)
            bias = _moba_bias(gate_t, i)
        return s_t, bias

    def attend(i, head, s_t, bias):
        vt_ref = vta_ref if head == 0 else vtb_ref
        kext = (i + 1) * tq
        diag = jnp.where(causal, s_t[i * tq:kext, :], NEG_INF)
        m = jnp.max(diag, axis=0, keepdims=True)
        if bias is not None:
            past = [s_t[n * tq:(n + 1) * tq, :] + bias[n:n + 1, :] for n in range(i)]
            bounds = [(n * tq, (n + 1) * tq) for n in range(i)]
        elif i > 0:
            past = [s_t[0:i * tq, :]]
            bounds = [(0, i * tq)]
        else:
            past, bounds = [], []
        for piece in past:
            m = jnp.maximum(m, jnp.max(piece, axis=0, keepdims=True))
        acc = jnp.dot(vt_ref[:, i * tq:kext], jnp.exp2(diag - m).astype(BF16),
                      preferred_element_type=F32)
        for piece, (lo, hi) in zip(past, bounds):
            acc = acc + jnp.dot(vt_ref[:, lo:hi], jnp.exp2(piece - m).astype(BF16),
                                preferred_element_type=F32)
        return acc

    items = [(i, head) for i in range(n_blocks) for head in (0, 1)]
    pending = [scores(*it) for it in items[:QK_LOOKAHEAD]]
    accs = []
    for j, (i, head) in enumerate(items):
        if j + QK_LOOKAHEAD < len(items):
            pending.append(scores(*items[j + QK_LOOKAHEAD]))
        accs.append(attend(i, head, *pending.pop(0)))
        if head == 1:
            out_t = jnp.where(out_row < HEAD_DIM, accs[0] / accs[0][HEAD_DIM:HEAD_DIM + 1, :],
                              accs[1] / accs[1][0:1, :])
            o_ref[i * tq:(i + 1) * tq, :] = out_t.T
            accs = []


def _attention(qkv3):
    b, seq, _ = qkv3.shape
    n_pairs = ATTN_WIDTH // LANES
    kern = functools.partial(_attn_kernel, seq=seq)
    return pl.pallas_call(
        kern,
        grid=(b, n_pairs),
        in_specs=[pl.BlockSpec((None, seq, LANES), lambda bi, j: (bi, 0, j)),
                  pl.BlockSpec((None, seq, LANES), lambda bi, j: (bi, 0, n_pairs + j)),
                  pl.BlockSpec((None, seq, LANES), lambda bi, j: (bi, 0, 2 * n_pairs + j))],
        out_specs=pl.BlockSpec((None, seq, LANES), lambda bi, j: (bi, 0, j)),
        out_shape=jax.ShapeDtypeStruct((b, seq, ATTN_WIDTH), F32),
        scratch_shapes=[pltpu.VMEM((seq, LANES), BF16),
                        pltpu.VMEM((LANES, seq), BF16),
                        pltpu.VMEM((LANES, seq), BF16),
                        pltpu.VMEM((seq // MOBA_BLOCK, LANES), F32)],
        compiler_params=pltpu.CompilerParams(dimension_semantics=("arbitrary", "arbitrary"),
                                             vmem_limit_bytes=VMEM_LIMIT),
        name="moba_attn",
    )(qkv3, qkv3, qkv3)


def _expand_heads(cols):
    rows = cols.shape[0]
    lane = lax.broadcasted_iota(jnp.int32, (rows, LANES), 1)
    out = []
    for pair in range(SSD_HEADS // 2):
        lo = jnp.broadcast_to(cols[:, 2 * pair:2 * pair + 1], (rows, LANES))
        hi = jnp.broadcast_to(cols[:, 2 * pair + 1:2 * pair + 2], (rows, LANES))
        out.append(jnp.where(lane < HEAD_DIM, lo, hi))
    return jnp.concatenate(out, axis=1)


def _cumsum_rows(x):
    n = x.shape[0]
    row = lax.broadcasted_iota(jnp.int32, x.shape, 0)
    k = 1
    while k < n:
        x = x + jnp.where(row >= k, pltpu.roll(x, k, 0), 0.0)
        k *= 2
    return x


def _ssd_kernel(xc_ref, zs_ref, dt_ref, alog_ref, dskip_ref, nw_ref, o_ref, state_ref, *, seq):
    L = SSD_CHUNK
    n_chunks = seq // L
    gw = SSD_WIDTH // SSD_GROUPS
    state_ref[...] = jnp.zeros(state_ref.shape, F32)

    r_i = lax.broadcasted_iota(jnp.int32, (L, L), 0)
    c_i = lax.broadcasted_iota(jnp.int32, (L, L), 1)
    tril = r_i >= c_i
    lane_w = lax.broadcasted_iota(jnp.int32, (L, SSD_WIDTH), 1)
    a_neg = -jnp.exp(alog_ref[...])

    def chunk(c, carry):
        r0 = pl.multiple_of(c * L, L)
        act = xc_ref[pl.ds(r0, L), :]
        xs = act[:, :SSD_WIDTH]
        b_in = act[:, SSD_WIDTH:SSD_WIDTH + SSD_GROUPS * SSD_STATE]
        c_in = act[:, SSD_WIDTH + SSD_GROUPS * SSD_STATE:]

        dt = dt_ref[pl.ds(r0, L), :]
        acum = _cumsum_rows(dt * a_neg)
        acum_t = acum.T
        acum_e = _expand_heads(acum)
        dt_e = _expand_heads(dt)
        last_e = acum_e[L - 1:L, :]
        xdt = xs * dt_e
        xdt_end = (xdt * jnp.exp(last_e - acum_e))

        state = state_ref[...]
        y = xs * dskip_ref[...]
        new_state = state * jnp.exp(last_e)
        y_off = jnp.zeros((L, SSD_WIDTH), F32)
        for g in range(SSD_GROUPS):
            in_g = (lane_w >= g * gw) & (lane_w < (g + 1) * gw)
            bg = b_in[:, g * SSD_STATE:(g + 1) * SSD_STATE]
            cg = c_in[:, g * SSD_STATE:(g + 1) * SSD_STATE].astype(BF16)
            cb = lax.dot_general(cg, bg.astype(BF16), (((1,), (1,)), ((), ())),
                                 preferred_element_type=F32)
            y_off = y_off + jnp.dot(cg, jnp.where(in_g[:SSD_STATE], state, 0.0).astype(BF16),
                                    preferred_element_type=F32)
            new_state = new_state + jnp.dot(bg.T.astype(BF16),
                                            jnp.where(in_g, xdt_end, 0.0).astype(BF16),
                                            preferred_element_type=F32)
            for hh in range(SSD_HEADS // SSD_GROUPS):
                h = g * (SSD_HEADS // SSD_GROUPS) + hh
                diff = jnp.broadcast_to(acum[:, h:h + 1], (L, L)) - acum_t[h:h + 1, :]
                decay = jnp.exp(jnp.where(tril, diff, NEG_INF))
                in_h = (lane_w >= h * HEAD_DIM) & (lane_w < (h + 1) * HEAD_DIM)
                y = y + jnp.dot((cb * decay).astype(BF16), jnp.where(in_h, xdt, 0.0).astype(BF16),
                                preferred_element_type=F32)
        y = y + y_off * jnp.exp(acum_e)
        state_ref[...] = new_state

        y = y * zs_ref[pl.ds(r0, L), :]
        y2 = y * y
        inv = jnp.zeros((L, SSD_WIDTH), F32)
        for g in range(SSD_GROUPS):
            in_g = (lane_w >= g * gw) & (lane_w < (g + 1) * gw)
            ms = jnp.sum(jnp.where(in_g, y2, 0.0), axis=-1, keepdims=True) * (1.0 / gw)
            inv = jnp.where(in_g, lax.rsqrt(ms + NORM_EPS), inv)
        o_ref[pl.ds(r0, L), :] = y * inv * nw_ref[...]
        return carry

    lax.fori_loop(0, n_chunks, chunk, 0, unroll=SSD_UNROLL)


def _ssd(xc3, zs3, dt3, alog, dskip, nw, layer):
    b, seq, _ = xc3.shape
    kern = functools.partial(_ssd_kernel, seq=seq)
    per_batch = lambda wd: pl.BlockSpec((None, seq, wd), lambda bi: (bi, 0, 0))
    return pl.pallas_call(
        kern,
        grid=(b,),
        in_specs=[per_batch(XBC_WIDTH), per_batch(SSD_WIDTH), per_batch(DT_PAD),
                  _resident((1, DT_PAD), layer), _resident((1, SSD_WIDTH), layer),
                  _resident((1, SSD_WIDTH), layer)],
        out_specs=per_batch(SSD_WIDTH),
        out_shape=jax.ShapeDtypeStruct((b, seq, SSD_WIDTH), F32),
        scratch_shapes=[pltpu.VMEM((SSD_STATE, SSD_WIDTH), F32)],
        compiler_params=pltpu.CompilerParams(dimension_semantics=("arbitrary",),
                                             vmem_limit_bytes=VMEM_LIMIT),
        name="ssd_mixer",
    )(xc3, zs3, dt3, alog, dskip, nw)


def _out_ffn_kernel(x_ref, ya_ref, ys_ref, yp_ref, wo_ref, g_ref, wgu_ref, wd_ref, gf_ref, o_ref,
                    *, final):
    x = x_ref[...]
    x = x + jnp.dot(ya_ref[...].astype(BF16), wo_ref[0:ATTN_WIDTH, :], preferred_element_type=F32)
    x = x + jnp.dot(ys_ref[...].astype(BF16), wo_ref[ATTN_WIDTH:ATTN_WIDTH + SSD_WIDTH, :],
                    preferred_element_type=F32)
    x = x + jnp.dot(yp_ref[...].astype(BF16), wo_ref[ATTN_WIDTH + SSD_WIDTH:, :],
                    preferred_element_type=F32)
    h = _rmsnorm(x, g_ref[...]).astype(BF16)
    gate = jnp.dot(h, wgu_ref[:, :FFN_HIDDEN], preferred_element_type=F32)
    up = jnp.dot(h, wgu_ref[:, FFN_HIDDEN:], preferred_element_type=F32)
    a = (_silu(gate) * up).astype(BF16)
    acc = x + jnp.dot(a, wd_ref[...], preferred_element_type=F32)
    if final:
        acc = _rmsnorm(acc, gf_ref[...])
    o_ref[...] = acc


def _out_ffn(x2, ya, ys, yp, wo, g, wgu, wd, gf, final, layer):
    m = x2.shape[0]
    kern = functools.partial(_out_ffn_kernel, final=final)
    rows = lambda wd_: pl.BlockSpec((ROW_TILE, wd_), lambda i: (i, 0))
    return pl.pallas_call(
        kern,
        grid=(m // ROW_TILE,),
        in_specs=[rows(D_MODEL), rows(ATTN_WIDTH), rows(SSD_WIDTH), rows(POOL_WIDTH),
                  _resident((MIX_WIDTH, D_MODEL)), _resident((1, D_MODEL), layer),
                  _resident((D_MODEL, 2 * FFN_HIDDEN)), _resident((FFN_HIDDEN, D_MODEL)),
                  _resident((1, D_MODEL))],
        out_specs=rows(D_MODEL),
        out_shape=jax.ShapeDtypeStruct((m, D_MODEL), F32),
        compiler_params=pltpu.CompilerParams(dimension_semantics=("arbitrary",),
                                             vmem_limit_bytes=VMEM_LIMIT),
        name="out_ffn",
    )(x2, ya, ys, yp, wo, g, wgu, wd, gf)


def _pad_lanes(v, width):
    return jnp.pad(v, [(0, 0)] * (v.ndim - 1) + [(0, width - v.shape[-1])])


def kernel(x, norm_mix, w_in, conv_w, conv_b, dt_bias, a_log, d_skip, ssd_norm, pool_w, pool_scale,
           w_out, norm_ffn, w_gate_up, w_down, norm_final):
    bsz, seq, d = x.shape
    depth = w_in.shape[0]
    m = bsz * seq
    dt_lo = 3 * ATTN_WIDTH + SSD_WIDTH + XBC_WIDTH
    w_in_p = jnp.concatenate(
        [w_in[:, :, :dt_lo], _pad_lanes(w_in[:, :, dt_lo:dt_lo + SSD_HEADS], DT_PAD),
         w_in[:, :, dt_lo + SSD_HEADS:]], axis=-1).astype(BF16)
    eye = jnp.eye(len(POOL_WINDOWS), dtype=F32)
    pool_bd = (eye[None, :, None, :, None] * pool_w[:, :, :, None, :]).reshape(
        depth, POOL_WIDTH, POOL_WIDTH).astype(BF16)
    row = lambda v: v.reshape(depth, 1, v.shape[-1])
    dtb_p = row(_pad_lanes(dt_bias, DT_PAD))
    alog_p = row(_pad_lanes(a_log, DT_PAD))
    dskip_e = row(jnp.repeat(d_skip, HEAD_DIM, axis=-1))
    g_mix, g_ffn, conv_b3 = row(norm_mix), row(norm_ffn), row(conv_b)
    pool_s3, ssd_norm3 = row(pool_scale), row(ssd_norm)

    x2 = x.reshape(m, d)
    gf = norm_final.reshape(1, d)
    for l in range(depth):
        qkv, zs, xc, dt, yp, wo_b, wgu_b, wd_b = _in_proj(
            x2, g_mix, w_in_p, conv_w, conv_b3, dtb_p, pool_bd, pool_s3, w_out, w_gate_up, w_down,
            seq, l)
        ya = _attention(qkv.reshape(bsz, seq, QKV_W))
        ys = _ssd(xc.reshape(bsz, seq, XBC_WIDTH), zs.reshape(bsz, seq, SSD_WIDTH),
                  dt.reshape(bsz, seq, DT_PAD), alog_p, dskip_e, ssd_norm3, l)
        x2 = _out_ffn(x2, ya.reshape(m, ATTN_WIDTH), ys.reshape(m, SSD_WIDTH), yp,
                      wo_b, g_ffn, wgu_b, wd_b, gf, final=(l == depth - 1), layer=l)
    return x2.reshape(bsz, seq, d)
```

```python
import functools
import math

import jax
import jax.numpy as jnp
from jax import lax
from jax.experimental import pallas as pl
from jax.experimental.pallas import tpu as pltpu

F32 = jnp.float32
BF16 = jnp.bfloat16

D_MODEL = 1024
ATTN_HEADS = 6
HEAD_DIM = 64
ATTN_WIDTH = ATTN_HEADS * HEAD_DIM
MOBA_BLOCK = 256
MOBA_TOPK = 3
SSD_HEADS = 6
SSD_WIDTH = SSD_HEADS * HEAD_DIM
SSD_GROUPS = 2
SSD_STATE = 128
SSD_CONV = 4
SSD_CHUNK = 128
XBC_WIDTH = SSD_WIDTH + 2 * SSD_GROUPS * SSD_STATE
POOL_WINDOWS = (2, 4, 8, 16)
POOL_WIDTH = 256
MIX_WIDTH = 1024
FFN_HIDDEN = 2816
NORM_EPS = 1e-6
NEG_INF = -1e30

LANES = 128
SUBLANES = 8
QKV_W = 3 * ATTN_WIDTH
DT_PAD = LANES
XBC_UNITS = XBC_WIDTH // LANES
QKV_UNITS = QKV_W // LANES
COL_Z = XBC_WIDTH + QKV_W
COL_DT = COL_Z + SSD_WIDTH
COL_P = COL_DT + DT_PAD
IN_PROJ_PAD = COL_P + POOL_WIDTH


def _mixed_unit(kind, u):
    if kind == "x":
        return 2 * u
    return 2 * u + 1 if u < XBC_UNITS else XBC_UNITS + u


ROW_TILE = 512
SSD_UNROLL = 4
QK_LOOKAHEAD = 4
VMEM_LIMIT = 56 * 1024 * 1024


def _resident(shape, layer=None):
    nd = len(shape)
    if layer is None:
        return pl.BlockSpec(shape, lambda *_: (0,) * nd, pipeline_mode=pl.Buffered(1))
    return pl.BlockSpec((None,) + tuple(shape), lambda *_: (layer,) + (0,) * nd,
                        pipeline_mode=pl.Buffered(1))


def _rmsnorm(x, g):
    ms = jnp.mean(x * x, axis=-1, keepdims=True)
    return x * lax.rsqrt(ms + NORM_EPS) * g


def _softplus(x):
    return jnp.maximum(x, 0.0) + jnp.log1p(jnp.exp(-jnp.abs(x)))


def _silu(x):
    return x * jax.nn.sigmoid(x)


def _rows_back(x, tail, k):
    xw = jnp.concatenate([tail, x], axis=0)
    return pltpu.roll(xw, k, 0)[tail.shape[0]:, :]


def _in_proj_kernel(x_ref, g_ref, w_ref, cw_ref, cb_ref, dtb_ref, pw_ref, ps_ref,
                    wo_f_ref, wgu_f_ref, wd_f_ref,
                    qkv_ref, zs_ref, xc_ref, dt_ref, yp_ref, wo_b_ref, wgu_b_ref, wd_b_ref,
                    ctail_ref, ptail_ref, *, tiles_per_seq):
    wo_b_ref[...] = wo_f_ref[...].astype(BF16)
    wgu_b_ref[...] = wgu_f_ref[...].astype(BF16)
    wd_b_ref[...] = wd_f_ref[...].astype(BF16)

    seq_tile = lax.rem(pl.program_id(0), tiles_per_seq)

    @pl.when(seq_tile == 0)
    def _():
        ctail_ref[...] = jnp.zeros(ctail_ref.shape, F32)
        ptail_ref[...] = jnp.zeros(ptail_ref.shape, F32)

    h = _rmsnorm(x_ref[...], g_ref[...]).astype(BF16)

    mix = jnp.dot(h, w_ref[:, 0:COL_Z], preferred_element_type=F32)
    p = jnp.dot(h, w_ref[:, COL_P:IN_PROJ_PAD], preferred_element_type=F32)
    z = jnp.dot(h, w_ref[:, COL_Z:COL_DT], preferred_element_type=F32)
    dt_raw = jnp.dot(h, w_ref[:, COL_DT:COL_P], preferred_element_type=F32)

    def tile(kind, u):
        k = _mixed_unit(kind, u)
        return mix[:, k * LANES:(k + 1) * LANES]

    cw = cw_ref[...]
    n_ct = ctail_ref.shape[0]
    for u in range(XBC_UNITS):
        cols = slice(u * LANES, (u + 1) * LANES)
        xu = tile("x", u)
        conv = cb_ref[:, cols] + cw[SSD_CONV - 1:SSD_CONV, cols] * xu
        for j in range(SSD_CONV - 1):
            conv = conv + cw[j:j + 1, cols] * _rows_back(xu, ctail_ref[:, cols], SSD_CONV - 1 - j)
        ctail_ref[:, cols] = xu[ROW_TILE - n_ct:, :]
        xc_ref[:, cols] = _silu(conv)
    for u in range(QKV_UNITS):
        qkv_ref[:, u * LANES:(u + 1) * LANES] = tile("q", u)

    n_tail = ptail_ref.shape[0]
    win = jnp.concatenate([ptail_ref[...], p], axis=0)
    ptail_ref[...] = p[ROW_TILE - n_tail:, :]
    pos = seq_tile * ROW_TILE + lax.broadcasted_iota(jnp.int32, p.shape, 0)
    lane = lax.broadcasted_iota(jnp.int32, p.shape, 1)
    total = jnp.zeros(p.shape, F32)
    cnt = jnp.ones(p.shape, F32)
    width = 1
    for gi, w in enumerate(POOL_WINDOWS):
        while width < w:
            win = win + pltpu.roll(win, width, 0)
            width *= 2
        in_g = (lane >= gi * HEAD_DIM) & (lane < (gi + 1) * HEAD_DIM)
        total = jnp.where(in_g, win[n_tail:, :], total)
        cnt = jnp.where(in_g, jnp.minimum(pos + 1, w).astype(F32), cnt)
    d = total / cnt - p
    zs_ref[...] = _silu(z)
    dt_ref[...] = _softplus(dt_raw + dtb_ref[...])
    yp_ref[...] = jnp.dot(d.astype(BF16), pw_ref[...], preferred_element_type=F32) * ps_ref[...]


def _slab_specs(rows, cols, steps, layer):
    bf16_rows = 2 * SUBLANES
    n_slabs = max(n for n in range(1, steps + 1)
                  if steps % n == 0 and rows % (n * bf16_rows) == 0)
    slab = rows // n_slabs
    per = steps // n_slabs
    src = pl.BlockSpec((None, slab, cols), lambda i: (layer, i // per, 0))
    dst = pl.BlockSpec((slab, cols), lambda i: (i // per, 0))
    return src, dst


def _in_proj(x2, g, w, cw, cb, dtb, pw, ps, wo, wgu, wd, seq, layer):
    m = x2.shape[0]
    steps = m // ROW_TILE
    outs = ((QKV_W, F32), (SSD_WIDTH, F32), (XBC_WIDTH, F32), (DT_PAD, F32), (POOL_WIDTH, F32))
    slabs = [_slab_specs(a.shape[1], a.shape[2], steps, layer) for a in (wo, wgu, wd)]
    kern = functools.partial(_in_proj_kernel, tiles_per_seq=seq // ROW_TILE)
    return pl.pallas_call(
        kern,
        grid=(steps,),
        in_specs=[pl.BlockSpec((ROW_TILE, D_MODEL), lambda i: (i, 0)),
                  _resident((1, D_MODEL), layer),
                  _resident((D_MODEL, IN_PROJ_PAD), layer),
                  _resident((SSD_CONV, XBC_WIDTH), layer), _resident((1, XBC_WIDTH), layer),
                  _resident((1, DT_PAD), layer),
                  _resident((POOL_WIDTH, POOL_WIDTH), layer), _resident((1, POOL_WIDTH), layer)]
        + [src for src, _ in slabs],
        out_specs=[pl.BlockSpec((ROW_TILE, wd_), lambda i: (i, 0)) for wd_, _ in outs]
        + [dst for _, dst in slabs],
        out_shape=[jax.ShapeDtypeStruct((m, wd_), dt) for wd_, dt in outs]
        + [jax.ShapeDtypeStruct(a.shape[1:], BF16) for a in (wo, wgu, wd)],
        scratch_shapes=[pltpu.VMEM((SUBLANES, XBC_WIDTH), F32),
                        pltpu.VMEM((2 * SUBLANES, POOL_WIDTH), F32)],
        compiler_params=pltpu.CompilerParams(dimension_semantics=("arbitrary",),
                                             vmem_limit_bytes=VMEM_LIMIT),
        name="in_proj",
    )(x2, g, w, cw, cb, dtb, pw, ps, wo, wgu, wd)


def _moba_bias(gate_t, blk):
    nb = gate_t.shape[0]
    blk_id = lax.broadcasted_iota(jnp.int32, gate_t.shape, 0)
    valid = blk_id < blk
    g = jnp.where(valid, gate_t, NEG_INF)
    rank = jnp.zeros(gate_t.shape, jnp.int32)
    for m in range(nb):
        gm = g[m:m + 1, :]
        beats = (gm > g) | ((gm == g) & (m < blk_id))
        rank = rank + beats.astype(jnp.int32)
    sel = valid & (rank < MOBA_TOPK)
    return jnp.where(sel, 0.0, NEG_INF).astype(F32)


def _attn_kernel(q_ref, k_ref, v_ref, o_ref, kb_ref, vta_ref, vtb_ref, kmean_ref, *, seq):
    n_blocks = seq // MOBA_BLOCK
    tq = MOBA_BLOCK
    k = k_ref[...]
    kmean_ref[...] = jnp.mean(k.reshape(n_blocks, MOBA_BLOCK, LANES), axis=1)
    kb_ref[...] = k.astype(BF16)
    vt = v_ref[...].T
    row = lax.broadcasted_iota(jnp.int32, vt.shape, 0)
    vta_ref[...] = jnp.where(row < HEAD_DIM, vt, jnp.where(row == HEAD_DIM, 1.0, 0.0)).astype(BF16)
    vtb_ref[...] = jnp.where(row >= HEAD_DIM, vt, jnp.where(row == 0, 1.0, 0.0)).astype(BF16)

    lane = lax.broadcasted_iota(jnp.int32, (tq, LANES), 1)
    half = tq // 2
    tri = (lax.broadcasted_iota(jnp.int32, (half, half), 0)
           <= lax.broadcasted_iota(jnp.int32, (half, half), 1))
    out_row = lax.broadcasted_iota(jnp.int32, (LANES, tq), 0)
    nt = (((1,), (1,)), ((), ()))
    scale2 = HEAD_DIM ** -0.5 * math.log2(math.e)

    class Tile:
        def __init__(self, i, head):
            self.i, self.head = i, head
            self.vt_ref = vta_ref if head == 0 else vtb_ref
            self.blocks, self.m, self.acc = [], None, None

        def score_step(self, n):
            i = self.i
            if n == 0:
                q = q_ref[i * tq:(i + 1) * tq, :]
                in_head = (lane < HEAD_DIM) if self.head == 0 else (lane >= HEAD_DIM)
                qh = jnp.where(in_head, q, 0.0)
                self.qs = (qh * scale2).astype(BF16)
                self.bias = None
                if i > MOBA_TOPK:
                    gate_t = lax.dot_general(kmean_ref[...], qh, nt, preferred_element_type=F32,
                                             precision=lax.Precision.HIGHEST)
                    self.bias = _moba_bias(gate_t, i)
            s = lax.dot_general(kb_ref[n * tq:(n + 1) * tq, :], self.qs, nt,
                                preferred_element_type=F32)
            if n == i:
                top_l = jnp.where(tri, s[:half, :half], NEG_INF)
                top_r = s[:half, half:]
                bot_r = jnp.where(tri, s[half:, half:], NEG_INF)
                bm = jnp.concatenate(
                    [jnp.max(top_l, axis=0, keepdims=True),
                     jnp.maximum(jnp.max(top_r, axis=0, keepdims=True),
                                 jnp.max(bot_r, axis=0, keepdims=True))], axis=1)
                self.blocks.append((top_l, top_r, bot_r))
            else:
                bm = jnp.max(s, axis=0, keepdims=True)
                if self.bias is not None:
                    bm = bm + self.bias[n:n + 1, :]
                self.blocks.append(s)
            self.m = bm if self.m is None else jnp.maximum(self.m, bm)

        def pv_step(self, n):
            m = self.m
            if n == self.i:
                top_l, top_r, bot_r = self.blocks[n]
                m_l, m_r = m[:, :half], m[:, half:]
                p_bot_r = jnp.exp2(bot_r - m_r)
                p = jnp.concatenate(
                    [jnp.concatenate([jnp.exp2(top_l - m_l), jnp.exp2(top_r - m_r)], axis=1),
                     jnp.concatenate([jnp.zeros_like(p_bot_r), p_bot_r], axis=1)], axis=0)
            else:
                if self.bias is not None:
                    m = m - self.bias[n:n + 1, :]
                p = jnp.exp2(self.blocks[n] - m)
            part = jnp.dot(self.vt_ref[:, n * tq:(n + 1) * tq], p.astype(BF16),
                           preferred_element_type=F32)
            self.acc = part if self.acc is None else self.acc + part

    tiles = [Tile(i, head) for i in range(n_blocks) for head in (0, 1)]
    for t in tiles[:QK_LOOKAHEAD]:
        for n in range(t.i + 1):
            t.score_step(n)
    done = []
    for j, t in enumerate(tiles):
        ahead = tiles[j + QK_LOOKAHEAD] if j + QK_LOOKAHEAD < len(tiles) else None
        n_ahead = ahead.i + 1 if ahead is not None else 0
        for n in range(max(t.i + 1, n_ahead)):
            if n < n_ahead:
                ahead.score_step(n)
            if n <= t.i:
                t.pv_step(n)
        done.append(t.acc)
        if t.head == 1:
            out_t = jnp.where(out_row < HEAD_DIM, done[0] / done[0][HEAD_DIM:HEAD_DIM + 1, :],
                              done[1] / done[1][0:1, :])
            o_ref[t.i * tq:(t.i + 1) * tq, :] = out_t.T
            done = []


def _attention(qkv3):
    b, seq, _ = qkv3.shape
    n_pairs = ATTN_WIDTH // LANES
    kern = functools.partial(_attn_kernel, seq=seq)
    return pl.pallas_call(
        kern,
        grid=(b, n_pairs),
        in_specs=[pl.BlockSpec((None, seq, LANES), lambda bi, j: (bi, 0, j)),
                  pl.BlockSpec((None, seq, LANES), lambda bi, j: (bi, 0, n_pairs + j)),
                  pl.BlockSpec((None, seq, LANES), lambda bi, j: (bi, 0, 2 * n_pairs + j))],
        out_specs=pl.BlockSpec((None, seq, LANES), lambda bi, j: (bi, 0, j)),
        out_shape=jax.ShapeDtypeStruct((b, seq, ATTN_WIDTH), F32),
        scratch_shapes=[pltpu.VMEM((seq, LANES), BF16),
                        pltpu.VMEM((LANES, seq), BF16),
                        pltpu.VMEM((LANES, seq), BF16),
                        pltpu.VMEM((seq // MOBA_BLOCK, LANES), F32)],
        compiler_params=pltpu.CompilerParams(dimension_semantics=("arbitrary", "arbitrary"),
                                             vmem_limit_bytes=VMEM_LIMIT),
        name="moba_attn",
    )(qkv3, qkv3, qkv3)


def _expand_heads(cols):
    rows = cols.shape[0]
    lane = lax.broadcasted_iota(jnp.int32, (rows, LANES), 1)
    out = []
    for pair in range(SSD_HEADS // 2):
        lo = jnp.broadcast_to(cols[:, 2 * pair:2 * pair + 1], (rows, LANES))
        hi = jnp.broadcast_to(cols[:, 2 * pair + 1:2 * pair + 2], (rows, LANES))
        out.append(jnp.where(lane < HEAD_DIM, lo, hi))
    return jnp.concatenate(out, axis=1)


def _cumsum_rows(x):
    n = x.shape[0]
    row = lax.broadcasted_iota(jnp.int32, x.shape, 0)
    k = 1
    while k < n:
        x = x + jnp.where(row >= k, pltpu.roll(x, k, 0), 0.0)
        k *= 2
    return x


def _ssd_kernel(xc_ref, zs_ref, dt_ref, alog_ref, dskip_ref, nw_ref, o_ref, state_ref, *, seq):
    L = SSD_CHUNK
    n_chunks = seq // L
    gw = SSD_WIDTH // SSD_GROUPS
    state_ref[...] = jnp.zeros(state_ref.shape, F32)

    r_i = lax.broadcasted_iota(jnp.int32, (L, L), 0)
    c_i = lax.broadcasted_iota(jnp.int32, (L, L), 1)
    tril = r_i >= c_i
    lane_w = lax.broadcasted_iota(jnp.int32, (L, SSD_WIDTH), 1)
    a_neg = -jnp.exp(alog_ref[...])

    def chunk(c, carry):
        r0 = pl.multiple_of(c * L, L)
        act = xc_ref[pl.ds(r0, L), :]
        xs = act[:, :SSD_WIDTH]
        b_in = act[:, SSD_WIDTH:SSD_WIDTH + SSD_GROUPS * SSD_STATE]
        c_in = act[:, SSD_WIDTH + SSD_GROUPS * SSD_STATE:]

        dt = dt_ref[pl.ds(r0, L), :]
        acum = _cumsum_rows(dt * a_neg)
        acum_t = acum.T
        acum_e = _expand_heads(acum)
        dt_e = _expand_heads(dt)
        last_e = acum_e[L - 1:L, :]
        xdt = xs * dt_e
        xdt_end = (xdt * jnp.exp(last_e - acum_e))

        state = state_ref[...]
        y = xs * dskip_ref[...]
        new_state = state * jnp.exp(last_e)
        y_off = jnp.zeros((L, SSD_WIDTH), F32)
        for g in range(SSD_GROUPS):
            in_g = (lane_w >= g * gw) & (lane_w < (g + 1) * gw)
            bg = b_in[:, g * SSD_STATE:(g + 1) * SSD_STATE]
            cg = c_in[:, g * SSD_STATE:(g + 1) * SSD_STATE].astype(BF16)
            cb = lax.dot_general(cg, bg.astype(BF16), (((1,), (1,)), ((), ())),
                                 preferred_element_type=F32)
            y_off = y_off + jnp.dot(cg, jnp.where(in_g[:SSD_STATE], state, 0.0).astype(BF16),
                                    preferred_element_type=F32)
            new_state = new_state + jnp.dot(bg.T.astype(BF16),
                                            jnp.where(in_g, xdt_end, 0.0).astype(BF16),
                                            preferred_element_type=F32)
            for hh in range(SSD_HEADS // SSD_GROUPS):
                h = g * (SSD_HEADS // SSD_GROUPS) + hh
                diff = jnp.broadcast_to(acum[:, h:h + 1], (L, L)) - acum_t[h:h + 1, :]
                decay = jnp.exp(jnp.where(tril, diff, NEG_INF))
                in_h = (lane_w >= h * HEAD_DIM) & (lane_w < (h + 1) * HEAD_DIM)
                y = y + jnp.dot((cb * decay).astype(BF16), jnp.where(in_h, xdt, 0.0).astype(BF16),
                                preferred_element_type=F32)
        y = y + y_off * jnp.exp(acum_e)
        state_ref[...] = new_state

        y = y * zs_ref[pl.ds(r0, L), :]
        y2 = y * y
        inv = jnp.zeros((L, SSD_WIDTH), F32)
        for g in range(SSD_GROUPS):
            in_g = (lane_w >= g * gw) & (lane_w < (g + 1) * gw)
            ms = jnp.sum(jnp.where(in_g, y2, 0.0), axis=-1, keepdims=True) * (1.0 / gw)
            inv = jnp.where(in_g, lax.rsqrt(ms + NORM_EPS), inv)
        o_ref[pl.ds(r0, L), :] = y * inv * nw_ref[...]
        return carry

    lax.fori_loop(0, n_chunks, chunk, 0, unroll=SSD_UNROLL)


def _ssd(xc3, zs3, dt3, alog, dskip, nw, layer):
    b, seq, _ = xc3.shape
    kern = functools.partial(_ssd_kernel, seq=seq)
    per_batch = lambda wd: pl.BlockSpec((None, seq, wd), lambda bi: (bi, 0, 0))
    return pl.pallas_call(
        kern,
        grid=(b,),
        in_specs=[per_batch(XBC_WIDTH), per_batch(SSD_WIDTH), per_batch(DT_PAD),
                  _resident((1, DT_PAD), layer), _resident((1, SSD_WIDTH), layer),
                  _resident((1, SSD_WIDTH), layer)],
        out_specs=per_batch(SSD_WIDTH),
        out_shape=jax.ShapeDtypeStruct((b, seq, SSD_WIDTH), F32),
        scratch_shapes=[pltpu.VMEM((SSD_STATE, SSD_WIDTH), F32)],
        compiler_params=pltpu.CompilerParams(dimension_semantics=("arbitrary",),
                                             vmem_limit_bytes=VMEM_LIMIT),
        name="ssd_mixer",
    )(xc3, zs3, dt3, alog, dskip, nw)


def _out_ffn_kernel(x_ref, ya_ref, ys_ref, yp_ref, wo_ref, g_ref, wgu_ref, wd_ref, gf_ref, o_ref,
                    *, final):
    y = jnp.concatenate([ya_ref[...], ys_ref[...], yp_ref[...]], axis=1).astype(BF16)
    x = x_ref[...] + jnp.dot(y, wo_ref[...], preferred_element_type=F32)
    h = _rmsnorm(x, g_ref[...]).astype(BF16)
    gate = jnp.dot(h, wgu_ref[:, :FFN_HIDDEN], preferred_element_type=F32)
    up = jnp.dot(h, wgu_ref[:, FFN_HIDDEN:], preferred_element_type=F32)
    a = (_silu(gate) * up).astype(BF16)
    acc = x + jnp.dot(a, wd_ref[...], preferred_element_type=F32)
    if final:
        acc = _rmsnorm(acc, gf_ref[...])
    o_ref[...] = acc


def _out_ffn(x2, ya, ys, yp, wo, g, wgu, wd, gf, final, layer):
    m = x2.shape[0]
    kern = functools.partial(_out_ffn_kernel, final=final)
    rows = lambda wd_: pl.BlockSpec((ROW_TILE, wd_), lambda i: (i, 0))
    return pl.pallas_call(
        kern,
        grid=(m // ROW_TILE,),
        in_specs=[rows(D_MODEL), rows(ATTN_WIDTH), rows(SSD_WIDTH), rows(POOL_WIDTH),
                  _resident((MIX_WIDTH, D_MODEL)), _resident((1, D_MODEL), layer),
                  _resident((D_MODEL, 2 * FFN_HIDDEN)), _resident((FFN_HIDDEN, D_MODEL)),
                  _resident((1, D_MODEL))],
        out_specs=rows(D_MODEL),
        out_shape=jax.ShapeDtypeStruct((m, D_MODEL), F32),
        compiler_params=pltpu.CompilerParams(dimension_semantics=("arbitrary",),
                                             vmem_limit_bytes=VMEM_LIMIT),
        name="out_ffn",
    )(x2, ya, ys, yp, wo, g, wgu, wd, gf)


def _pad_lanes(v, width):
    return jnp.pad(v, [(0, 0)] * (v.ndim - 1) + [(0, width - v.shape[-1])])


def kernel(x, norm_mix, w_in, conv_w, conv_b, dt_bias, a_log, d_skip, ssd_norm, pool_w, pool_scale,
           w_out, norm_ffn, w_gate_up, w_down, norm_final):
    bsz, seq, d = x.shape
    depth = w_in.shape[0]
    m = bsz * seq
    z_lo = QKV_W
    x_lo = z_lo + SSD_WIDTH
    dt_lo = x_lo + XBC_WIDTH
    p_lo = dt_lo + SSD_HEADS
    unit = lambda lo, u: w_in[:, :, lo + u * LANES:lo + (u + 1) * LANES]
    order = sorted([("x", u) for u in range(XBC_UNITS)] + [("q", u) for u in range(QKV_UNITS)],
                   key=lambda ku: _mixed_unit(*ku))
    w_in_p = jnp.concatenate(
        [unit(x_lo if kind == "x" else 0, u) for kind, u in order]
        + [w_in[:, :, z_lo:x_lo], _pad_lanes(w_in[:, :, dt_lo:p_lo], DT_PAD), w_in[:, :, p_lo:]],
        axis=-1).astype(BF16)
    eye = jnp.eye(len(POOL_WINDOWS), dtype=F32)
    pool_bd = (eye[None, :, None, :, None] * pool_w[:, :, :, None, :]).reshape(
        depth, POOL_WIDTH, POOL_WIDTH).astype(BF16)
    row = lambda v: v.reshape(depth, 1, v.shape[-1])
    dtb_p = row(_pad_lanes(dt_bias, DT_PAD))
    alog_p = row(_pad_lanes(a_log, DT_PAD))
    dskip_e = row(jnp.repeat(d_skip, HEAD_DIM, axis=-1))
    g_mix, g_ffn, conv_b3 = row(norm_mix), row(norm_ffn), row(conv_b)
    pool_s3, ssd_norm3 = row(pool_scale), row(ssd_norm)

    x2 = x.reshape(m, d)
    gf = norm_final.reshape(1, d)
    for l in range(depth):
        qkv, zs, xc, dt, yp, wo_b, wgu_b, wd_b = _in_proj(
            x2, g_mix, w_in_p, conv_w, conv_b3, dtb_p, pool_bd, pool_s3, w_out, w_gate_up, w_down,
            seq, l)
        ya = _attention(qkv.reshape(bsz, seq, QKV_W))
        ys = _ssd(xc.reshape(bsz, seq, XBC_WIDTH), zs.reshape(bsz, seq, SSD_WIDTH),
                  dt.reshape(bsz, seq, DT_PAD), alog_p, dskip_e, ssd_norm3, l)
        x2 = _out_ffn(x2, ya.reshape(m, ATTN_WIDTH), ys.reshape(m, SSD_WIDTH), yp,
                      wo_b, g_ffn, wgu_b, wd_b, gf, final=(l == depth - 1), layer=l)
    return x2.reshape(bsz, seq, d)
```

```python
import functools
import math

import jax
import jax.numpy as jnp
from jax import lax
from jax.experimental import pallas as pl
from jax.experimental.pallas import tpu as pltpu

F32 = jnp.float32
BF16 = jnp.bfloat16

D_MODEL = 1024
ATTN_HEADS = 6
HEAD_DIM = 64
ATTN_WIDTH = ATTN_HEADS * HEAD_DIM
MOBA_BLOCK = 256
MOBA_TOPK = 3
SSD_HEADS = 6
SSD_WIDTH = SSD_HEADS * HEAD_DIM
SSD_GROUPS = 2
SSD_STATE = 128
SSD_CONV = 4
SSD_CHUNK = 128
XBC_WIDTH = SSD_WIDTH + 2 * SSD_GROUPS * SSD_STATE
POOL_WINDOWS = (2, 4, 8, 16)
POOL_WIDTH = 256
MIX_WIDTH = 1024
FFN_HIDDEN = 2816
NORM_EPS = 1e-6
NEG_INF = -1e30

LANES = 128
SUBLANES = 8
QKV_W = 3 * ATTN_WIDTH
DT_PAD = LANES
XBC_UNITS = XBC_WIDTH // LANES
QKV_UNITS = QKV_W // LANES
COL_Z = XBC_WIDTH + QKV_W
COL_DT = COL_Z + SSD_WIDTH
COL_P = COL_DT + DT_PAD
IN_PROJ_PAD = COL_P + POOL_WIDTH


def _mixed_unit(kind, u):
    if kind == "x":
        return 2 * u
    return 2 * u + 1 if u < XBC_UNITS else XBC_UNITS + u


ROW_TILE = 512
SSD_UNROLL = 4
QK_LOOKAHEAD = 4
VMEM_LIMIT = 56 * 1024 * 1024


def _resident(shape, layer=None):
    nd = len(shape)
    if layer is None:
        return pl.BlockSpec(shape, lambda *_: (0,) * nd, pipeline_mode=pl.Buffered(1))
    return pl.BlockSpec((None,) + tuple(shape), lambda *_: (layer,) + (0,) * nd,
                        pipeline_mode=pl.Buffered(1))


def _rmsnorm(x, g):
    ms = jnp.mean(x * x, axis=-1, keepdims=True)
    return x * lax.rsqrt(ms + NORM_EPS) * g


def _softplus(x):
    return jnp.maximum(x, 0.0) + jnp.log1p(jnp.exp(-jnp.abs(x)))


def _silu(x):
    return x * jax.nn.sigmoid(x)


def _rows_back(x, tail, k):
    xw = jnp.concatenate([tail, x], axis=0)
    return pltpu.roll(xw, k, 0)[tail.shape[0]:, :]


def _expand_heads(cols):
    rows = cols.shape[0]
    lane = lax.broadcasted_iota(jnp.int32, (rows, LANES), 1)
    out = []
    for pair in range(SSD_HEADS // 2):
        lo = jnp.broadcast_to(cols[:, 2 * pair:2 * pair + 1], (rows, LANES))
        hi = jnp.broadcast_to(cols[:, 2 * pair + 1:2 * pair + 2], (rows, LANES))
        out.append(jnp.where(lane < HEAD_DIM, lo, hi))
    return jnp.concatenate(out, axis=1)


XH_SKIP = SSD_WIDTH
XH_B = 2 * SSD_WIDTH
XH_C = XH_B + SSD_GROUPS * SSD_STATE
XH_WIDTH = XH_C + SSD_GROUPS * SSD_STATE


def _in_proj_kernel(x_ref, g_ref, w_ref, cw_ref, cb_ref, dtb_ref, dskip_ref, pw_ref, ps_ref,
                    wo_f_ref, wgu_f_ref, wd_f_ref,
                    qkv_ref, zs_ref, xh_ref, dt_ref, yp_ref, wo_b_ref, wgu_b_ref, wd_b_ref,
                    ctail_ref, ptail_ref, *, tiles_per_seq):
    wo_b_ref[...] = wo_f_ref[...].astype(BF16)
    wgu_b_ref[...] = wgu_f_ref[...].astype(BF16)
    wd_b_ref[...] = wd_f_ref[...].astype(BF16)

    seq_tile = lax.rem(pl.program_id(0), tiles_per_seq)

    @pl.when(seq_tile == 0)
    def _():
        ctail_ref[...] = jnp.zeros(ctail_ref.shape, F32)
        ptail_ref[...] = jnp.zeros(ptail_ref.shape, F32)

    h = _rmsnorm(x_ref[...], g_ref[...]).astype(BF16)

    dt_raw = jnp.dot(h, w_ref[:, COL_DT:COL_P], preferred_element_type=F32)
    mix = jnp.dot(h, w_ref[:, 0:COL_Z], preferred_element_type=F32)
    p = jnp.dot(h, w_ref[:, COL_P:IN_PROJ_PAD], preferred_element_type=F32)
    z = jnp.dot(h, w_ref[:, COL_Z:COL_DT], preferred_element_type=F32)

    def tile(kind, u):
        k = _mixed_unit(kind, u)
        return mix[:, k * LANES:(k + 1) * LANES]

    dt = _softplus(dt_raw + dtb_ref[...])
    dt_ref[...] = dt
    dt_e = _expand_heads(dt)

    cw = cw_ref[...]
    n_ct = ctail_ref.shape[0]
    for u in range(XBC_UNITS):
        cols = slice(u * LANES, (u + 1) * LANES)
        xu = tile("x", u)
        conv = cb_ref[:, cols] + cw[SSD_CONV - 1:SSD_CONV, cols] * xu
        for j in range(SSD_CONV - 1):
            conv = conv + cw[j:j + 1, cols] * _rows_back(xu, ctail_ref[:, cols], SSD_CONV - 1 - j)
        ctail_ref[:, cols] = xu[ROW_TILE - n_ct:, :]
        act = _silu(conv)
        if u < SSD_WIDTH // LANES:
            xh_ref[:, cols] = act * dt_e[:, cols]
            xh_ref[:, XH_SKIP + u * LANES:XH_SKIP + (u + 1) * LANES] = act * dskip_ref[:, cols]
        else:
            lo = XH_B + (u - SSD_WIDTH // LANES) * LANES
            xh_ref[:, lo:lo + LANES] = act
    for u in range(QKV_UNITS):
        qkv_ref[:, u * LANES:(u + 1) * LANES] = tile("q", u)

    n_tail = ptail_ref.shape[0]
    win = jnp.concatenate([ptail_ref[...], p], axis=0)
    ptail_ref[...] = p[ROW_TILE - n_tail:, :]
    pos = seq_tile * ROW_TILE + lax.broadcasted_iota(jnp.int32, p.shape, 0)
    lane = lax.broadcasted_iota(jnp.int32, p.shape, 1)
    total = jnp.zeros(p.shape, F32)
    cnt = jnp.ones(p.shape, F32)
    width = 1
    for gi, w in enumerate(POOL_WINDOWS):
        while width < w:
            win = win + pltpu.roll(win, width, 0)
            width *= 2
        in_g = (lane >= gi * HEAD_DIM) & (lane < (gi + 1) * HEAD_DIM)
        total = jnp.where(in_g, win[n_tail:, :], total)
        cnt = jnp.where(in_g, jnp.minimum(pos + 1, w).astype(F32), cnt)
    d = total / cnt - p
    zs_ref[...] = _silu(z)
    yp_ref[...] = jnp.dot(d.astype(BF16), pw_ref[...], preferred_element_type=F32) * ps_ref[...]


def _slab_specs(rows, cols, steps, layer):
    bf16_rows = 2 * SUBLANES
    n_slabs = max(n for n in range(1, steps + 1)
                  if steps % n == 0 and rows % (n * bf16_rows) == 0)
    slab = rows // n_slabs
    per = steps // n_slabs
    src = pl.BlockSpec((None, slab, cols), lambda i: (layer, i // per, 0))
    dst = pl.BlockSpec((slab, cols), lambda i: (i // per, 0))
    return src, dst


def _in_proj(x2, g, w, cw, cb, dtb, dskip, pw, ps, wo, wgu, wd, seq, layer):
    m = x2.shape[0]
    steps = m // ROW_TILE
    outs = ((QKV_W, F32), (SSD_WIDTH, F32), (XH_WIDTH, F32), (DT_PAD, F32), (POOL_WIDTH, F32))
    slabs = [_slab_specs(a.shape[1], a.shape[2], steps, layer) for a in (wo, wgu, wd)]
    kern = functools.partial(_in_proj_kernel, tiles_per_seq=seq // ROW_TILE)
    return pl.pallas_call(
        kern,
        grid=(steps,),
        in_specs=[pl.BlockSpec((ROW_TILE, D_MODEL), lambda i: (i, 0)),
                  _resident((1, D_MODEL), layer),
                  _resident((D_MODEL, IN_PROJ_PAD), layer),
                  _resident((SSD_CONV, XBC_WIDTH), layer), _resident((1, XBC_WIDTH), layer),
                  _resident((1, DT_PAD), layer), _resident((1, SSD_WIDTH), layer),
                  _resident((POOL_WIDTH, POOL_WIDTH), layer), _resident((1, POOL_WIDTH), layer)]
        + [src for src, _ in slabs],
        out_specs=[pl.BlockSpec((ROW_TILE, wd_), lambda i: (i, 0)) for wd_, _ in outs]
        + [dst for _, dst in slabs],
        out_shape=[jax.ShapeDtypeStruct((m, wd_), dt) for wd_, dt in outs]
        + [jax.ShapeDtypeStruct(a.shape[1:], BF16) for a in (wo, wgu, wd)],
        scratch_shapes=[pltpu.VMEM((SUBLANES, XBC_WIDTH), F32),
                        pltpu.VMEM((2 * SUBLANES, POOL_WIDTH), F32)],
        compiler_params=pltpu.CompilerParams(dimension_semantics=("arbitrary",),
                                             vmem_limit_bytes=VMEM_LIMIT),
        name="in_proj",
    )(x2, g, w, cw, cb, dtb, dskip, pw, ps, wo, wgu, wd)


def _moba_bias(gate_t, blk):
    nb = gate_t.shape[0]
    blk_id = lax.broadcasted_iota(jnp.int32, gate_t.shape, 0)
    valid = blk_id < blk
    g = jnp.where(valid, gate_t, NEG_INF)
    rank = jnp.zeros(gate_t.shape, jnp.int32)
    for m in range(nb):
        gm = g[m:m + 1, :]
        beats = (gm > g) | ((gm == g) & (m < blk_id))
        rank = rank + beats.astype(jnp.int32)
    sel = valid & (rank < MOBA_TOPK)
    return jnp.where(sel, 0.0, NEG_INF).astype(F32)


def _attn_kernel(q_ref, k_ref, v_ref, o_ref, kb_ref, vta_ref, vtb_ref, kmean_ref, *, seq):
    n_blocks = seq // MOBA_BLOCK
    tq = MOBA_BLOCK
    k = k_ref[...]
    kmean_ref[...] = jnp.mean(k.reshape(n_blocks, MOBA_BLOCK, LANES), axis=1)
    kb_ref[...] = k.astype(BF16)
    vt = v_ref[...].T
    row = lax.broadcasted_iota(jnp.int32, vt.shape, 0)
    vta_ref[...] = jnp.where(row < HEAD_DIM, vt, jnp.where(row == HEAD_DIM, 1.0, 0.0)).astype(BF16)
    vtb_ref[...] = jnp.where(row >= HEAD_DIM, vt, jnp.where(row == 0, 1.0, 0.0)).astype(BF16)

    lane = lax.broadcasted_iota(jnp.int32, (tq, LANES), 1)
    half = tq // 2
    tri = (lax.broadcasted_iota(jnp.int32, (half, half), 0)
           <= lax.broadcasted_iota(jnp.int32, (half, half), 1))
    out_row = lax.broadcasted_iota(jnp.int32, (LANES, tq), 0)
    nt = (((1,), (1,)), ((), ()))
    scale2 = HEAD_DIM ** -0.5 * math.log2(math.e)

    class Tile:
        def __init__(self, i, head):
            self.i, self.head = i, head
            self.vt_ref = vta_ref if head == 0 else vtb_ref
            self.blocks, self.m, self.acc = [], None, None

        def score_step(self, n):
            i = self.i
            if n == 0:
                q = q_ref[i * tq:(i + 1) * tq, :]
                in_head = (lane < HEAD_DIM) if self.head == 0 else (lane >= HEAD_DIM)
                qh = jnp.where(in_head, q, 0.0)
                self.qs = (qh * scale2).astype(BF16)
                self.bias = None
                if i > MOBA_TOPK:
                    gate_t = lax.dot_general(kmean_ref[...], qh, nt, preferred_element_type=F32,
                                             precision=lax.Precision.HIGHEST)
                    self.bias = _moba_bias(gate_t, i)
            s = lax.dot_general(kb_ref[n * tq:(n + 1) * tq, :], self.qs, nt,
                                preferred_element_type=F32)
            if n == i:
                top_l = jnp.where(tri, s[:half, :half], NEG_INF)
                top_r = s[:half, half:]
                bot_r = jnp.where(tri, s[half:, half:], NEG_INF)
                bm = jnp.concatenate(
                    [jnp.max(top_l, axis=0, keepdims=True),
                     jnp.maximum(jnp.max(top_r, axis=0, keepdims=True),
                                 jnp.max(bot_r, axis=0, keepdims=True))], axis=1)
                self.blocks.append((top_l, top_r, bot_r))
            else:
                bm = jnp.max(s, axis=0, keepdims=True)
                if self.bias is not None:
                    bm = bm + self.bias[n:n + 1, :]
                self.blocks.append(s)
            self.m = bm if self.m is None else jnp.maximum(self.m, bm)

        def pv_step(self, n):
            m = self.m
            if n == self.i:
                top_l, top_r, bot_r = self.blocks[n]
                m_l, m_r = m[:, :half], m[:, half:]
                p_bot_r = jnp.exp2(bot_r - m_r)
                p = jnp.concatenate(
                    [jnp.concatenate([jnp.exp2(top_l - m_l), jnp.exp2(top_r - m_r)], axis=1),
                     jnp.concatenate([jnp.zeros_like(p_bot_r), p_bot_r], axis=1)], axis=0)
            else:
                if self.bias is not None:
                    m = m - self.bias[n:n + 1, :]
                p = jnp.exp2(self.blocks[n] - m)
            part = jnp.dot(self.vt_ref[:, n * tq:(n + 1) * tq], p.astype(BF16),
                           preferred_element_type=F32)
            self.acc = part if self.acc is None else self.acc + part

    tiles = [Tile(i, head) for i in range(n_blocks) for head in (0, 1)]
    for t in tiles[:QK_LOOKAHEAD]:
        for n in range(t.i + 1):
            t.score_step(n)
    done = []
    for j, t in enumerate(tiles):
        ahead = tiles[j + QK_LOOKAHEAD] if j + QK_LOOKAHEAD < len(tiles) else None
        n_ahead = ahead.i + 1 if ahead is not None else 0
        for n in range(max(t.i + 1, n_ahead)):
            if n < n_ahead:
                ahead.score_step(n)
            if n <= t.i:
                t.pv_step(n)
        done.append(t.acc)
        if t.head == 1:
            out_t = jnp.where(out_row < HEAD_DIM, done[0] / done[0][HEAD_DIM:HEAD_DIM + 1, :],
                              done[1] / done[1][0:1, :])
            o_ref[t.i * tq:(t.i + 1) * tq, :] = out_t.T
            done = []


def _attention(qkv3):
    b, seq, _ = qkv3.shape
    n_pairs = ATTN_WIDTH // LANES
    kern = functools.partial(_attn_kernel, seq=seq)
    return pl.pallas_call(
        kern,
        grid=(b, n_pairs),
        in_specs=[pl.BlockSpec((None, seq, LANES), lambda bi, j: (bi, 0, j)),
                  pl.BlockSpec((None, seq, LANES), lambda bi, j: (bi, 0, n_pairs + j)),
                  pl.BlockSpec((None, seq, LANES), lambda bi, j: (bi, 0, 2 * n_pairs + j))],
        out_specs=pl.BlockSpec((None, seq, LANES), lambda bi, j: (bi, 0, j)),
        out_shape=jax.ShapeDtypeStruct((b, seq, ATTN_WIDTH), F32),
        scratch_shapes=[pltpu.VMEM((seq, LANES), BF16),
                        pltpu.VMEM((LANES, seq), BF16),
                        pltpu.VMEM((LANES, seq), BF16),
                        pltpu.VMEM((seq // MOBA_BLOCK, LANES), F32)],
        compiler_params=pltpu.CompilerParams(dimension_semantics=("arbitrary", "arbitrary"),
                                             vmem_limit_bytes=VMEM_LIMIT),
        name="moba_attn",
    )(qkv3, qkv3, qkv3)


def _cumsum_rows(x):
    n = x.shape[0]
    row = lax.broadcasted_iota(jnp.int32, x.shape, 0)
    k = 1
    while k < n:
        x = x + jnp.where(row >= k, pltpu.roll(x, k, 0), 0.0)
        k *= 2
    return x


def _ssd_kernel(xh_ref, zs_ref, dt_ref, alog_ref, nw_ref, o_ref, state_ref, *, seq):
    L = SSD_CHUNK
    n_chunks = seq // L
    gw = SSD_WIDTH // SSD_GROUPS
    per_group = SSD_HEADS // SSD_GROUPS
    state_ref[...] = jnp.zeros(state_ref.shape, F32)

    r_i = lax.broadcasted_iota(jnp.int32, (L, L), 0)
    c_i = lax.broadcasted_iota(jnp.int32, (L, L), 1)
    tril = r_i >= c_i
    lane_w = lax.broadcasted_iota(jnp.int32, (L, SSD_WIDTH), 1)
    low_half = lax.broadcasted_iota(jnp.int32, (L, LANES), 1) < HEAD_DIM
    a_neg = -jnp.exp(alog_ref[...])
    nt = (((1,), (1,)), ((), ()))

    def chunk(c, carry):
        r0 = pl.multiple_of(c * L, L)
        b_in = xh_ref[pl.ds(r0, L), XH_B:XH_C]
        c_in = xh_ref[pl.ds(r0, L), XH_C:XH_WIDTH]

        xdt = xh_ref[pl.ds(r0, L), 0:XH_SKIP]
        dt = dt_ref[pl.ds(r0, L), :]
        acum = _cumsum_rows(dt * a_neg)
        acum_t = acum.T
        acum_b = [jnp.broadcast_to(acum[:, h:h + 1], (L, LANES)) for h in range(SSD_HEADS)]
        acum_e = jnp.concatenate([jnp.where(low_half, acum_b[2 * pr], acum_b[2 * pr + 1])
                                  for pr in range(SSD_HEADS // 2)], axis=1)
        last_e = acum_e[L - 1:L, :]
        xdt_end = xdt * jnp.exp(last_e - acum_e)

        state = state_ref[...]
        new_state = state * jnp.exp(last_e)
        y_off = jnp.zeros((L, SSD_WIDTH), F32)
        cb = []
        for g in range(SSD_GROUPS):
            in_g = (lane_w >= g * gw) & (lane_w < (g + 1) * gw)
            bg = b_in[:, g * SSD_STATE:(g + 1) * SSD_STATE]
            cg = c_in[:, g * SSD_STATE:(g + 1) * SSD_STATE].astype(BF16)
            cb.append(lax.dot_general(cg, bg.astype(BF16), nt, preferred_element_type=F32))
            y_off = y_off + jnp.dot(cg, jnp.where(in_g[:SSD_STATE], state, 0.0).astype(BF16),
                                    preferred_element_type=F32)
            new_state = new_state + jnp.dot(bg.T.astype(BF16),
                                            jnp.where(in_g, xdt_end, 0.0).astype(BF16),
                                            preferred_element_type=F32)
        state_ref[...] = new_state

        y_diag = []
        for pr in range(SSD_HEADS // 2):
            xp = xdt[:, pr * LANES:(pr + 1) * LANES].astype(BF16)
            halves = []
            for h in (2 * pr, 2 * pr + 1):
                decay = jnp.exp(jnp.where(tril, acum_b[h] - acum_t[h:h + 1, :], NEG_INF))
                halves.append(jnp.dot((cb[h // per_group] * decay).astype(BF16), xp,
                                      preferred_element_type=F32))
            y_diag.append(jnp.where(low_half, halves[0], halves[1]))
        y = (jnp.concatenate(y_diag, axis=1) + xh_ref[pl.ds(r0, L), XH_SKIP:XH_B]
             + y_off * jnp.exp(acum_e))

        y = y * zs_ref[pl.ds(r0, L), :]
        y2 = y * y
        inv = jnp.zeros((L, SSD_WIDTH), F32)
        for g in range(SSD_GROUPS):
            in_g = (lane_w >= g * gw) & (lane_w < (g + 1) * gw)
            ms = jnp.sum(jnp.where(in_g, y2, 0.0), axis=-1, keepdims=True) * (1.0 / gw)
            inv = jnp.where(in_g, lax.rsqrt(ms + NORM_EPS), inv)
        o_ref[pl.ds(r0, L), :] = y * inv * nw_ref[...]
        return carry

    lax.fori_loop(0, n_chunks, chunk, 0, unroll=SSD_UNROLL)


def _ssd(xh3, zs3, dt3, alog, nw, layer):
    b, seq, _ = xh3.shape
    kern = functools.partial(_ssd_kernel, seq=seq)
    per_batch = lambda wd: pl.BlockSpec((None, seq, wd), lambda bi: (bi, 0, 0))
    return pl.pallas_call(
        kern,
        grid=(b,),
        in_specs=[per_batch(XH_WIDTH), per_batch(SSD_WIDTH), per_batch(DT_PAD),
                  _resident((1, DT_PAD), layer), _resident((1, SSD_WIDTH), layer)],
        out_specs=per_batch(SSD_WIDTH),
        out_shape=jax.ShapeDtypeStruct((b, seq, SSD_WIDTH), F32),
        scratch_shapes=[pltpu.VMEM((SSD_STATE, SSD_WIDTH), F32)],
        compiler_params=pltpu.CompilerParams(dimension_semantics=("arbitrary",),
                                             vmem_limit_bytes=VMEM_LIMIT),
        name="ssd_mixer",
    )(xh3, zs3, dt3, alog, nw)


def _out_ffn_kernel(x_ref, ya_ref, ys_ref, yp_ref, wo_ref, g_ref, wgu_ref, wd_ref, gf_ref, o_ref,
                    *, final):
    y = jnp.concatenate([ya_ref[...], ys_ref[...], yp_ref[...]], axis=1).astype(BF16)
    x = x_ref[...] + jnp.dot(y, wo_ref[...], preferred_element_type=F32)
    h = _rmsnorm(x, g_ref[...]).astype(BF16)
    gate = jnp.dot(h, wgu_ref[:, :FFN_HIDDEN], preferred_element_type=F32)
    up = jnp.dot(h, wgu_ref[:, FFN_HIDDEN:], preferred_element_type=F32)
    a = (_silu(gate) * up).astype(BF16)
    acc = x + jnp.dot(a, wd_ref[...], preferred_element_type=F32)
    if final:
        acc = _rmsnorm(acc, gf_ref[...])
    o_ref[...] = acc


def _out_ffn(x2, ya, ys, yp, wo, g, wgu, wd, gf, final, layer):
    m = x2.shape[0]
    kern = functools.partial(_out_ffn_kernel, final=final)
    rows = lambda wd_: pl.BlockSpec((ROW_TILE, wd_), lambda i: (i, 0))
    return pl.pallas_call(
        kern,
        grid=(m // ROW_TILE,),
        in_specs=[rows(D_MODEL), rows(ATTN_WIDTH), rows(SSD_WIDTH), rows(POOL_WIDTH),
                  _resident((MIX_WIDTH, D_MODEL)), _resident((1, D_MODEL), layer),
                  _resident((D_MODEL, 2 * FFN_HIDDEN)), _resident((FFN_HIDDEN, D_MODEL)),
                  _resident((1, D_MODEL))],
        out_specs=rows(D_MODEL),
        out_shape=jax.ShapeDtypeStruct((m, D_MODEL), F32),
        compiler_params=pltpu.CompilerParams(dimension_semantics=("arbitrary",),
                                             vmem_limit_bytes=VMEM_LIMIT),
        name="out_ffn",
    )(x2, ya, ys, yp, wo, g, wgu, wd, gf)


def _pad_lanes(v, width):
    return jnp.pad(v, [(0, 0)] * (v.ndim - 1) + [(0, width - v.shape[-1])])


def kernel(x, norm_mix, w_in, conv_w, conv_b, dt_bias, a_log, d_skip, ssd_norm, pool_w, pool_scale,
           w_out, norm_ffn, w_gate_up, w_down, norm_final):
    bsz, seq, d = x.shape
    depth = w_in.shape[0]
    m = bsz * seq
    z_lo = QKV_W
    x_lo = z_lo + SSD_WIDTH
    dt_lo = x_lo + XBC_WIDTH
    p_lo = dt_lo + SSD_HEADS
    unit = lambda lo, u: w_in[:, :, lo + u * LANES:lo + (u + 1) * LANES]
    order = sorted([("x", u) for u in range(XBC_UNITS)] + [("q", u) for u in range(QKV_UNITS)],
                   key=lambda ku: _mixed_unit(*ku))
    w_in_p = jnp.concatenate(
        [unit(x_lo if kind == "x" else 0, u) for kind, u in order]
        + [w_in[:, :, z_lo:x_lo], _pad_lanes(w_in[:, :, dt_lo:p_lo], DT_PAD), w_in[:, :, p_lo:]],
        axis=-1).astype(BF16)
    eye = jnp.eye(len(POOL_WINDOWS), dtype=F32)
    pool_bd = (eye[None, :, None, :, None] * pool_w[:, :, :, None, :]).reshape(
        depth, POOL_WIDTH, POOL_WIDTH).astype(BF16)
    row = lambda v: v.reshape(depth, 1, v.shape[-1])
    dtb_p = row(_pad_lanes(dt_bias, DT_PAD))
    alog_p = row(_pad_lanes(a_log, DT_PAD))
    dskip_e = row(jnp.repeat(d_skip, HEAD_DIM, axis=-1))
    g_mix, g_ffn, conv_b3 = row(norm_mix), row(norm_ffn), row(conv_b)
    pool_s3, ssd_norm3 = row(pool_scale), row(ssd_norm)

    x2 = x.reshape(m, d)
    gf = norm_final.reshape(1, d)
    for l in range(depth):
        qkv, zs, xh, dt, yp, wo_b, wgu_b, wd_b = _in_proj(
            x2, g_mix, w_in_p, conv_w, conv_b3, dtb_p, dskip_e, pool_bd, pool_s3,
            w_out, w_gate_up, w_down, seq, l)
        ya = _attention(qkv.reshape(bsz, seq, QKV_W))
        ys = _ssd(xh.reshape(bsz, seq, XH_WIDTH), zs.reshape(bsz, seq, SSD_WIDTH),
                  dt.reshape(bsz, seq, DT_PAD), alog_p, ssd_norm3, l)
        x2 = _out_ffn(x2, ya.reshape(m, ATTN_WIDTH), ys.reshape(m, SSD_WIDTH), yp,
                      wo_b, g_ffn, wgu_b, wd_b, gf, final=(l == depth - 1), layer=l)
    return x2.reshape(bsz, seq, d)
```

```python
import functools
import math

import jax
import jax.numpy as jnp
from jax import lax
from jax.experimental import pallas as pl
from jax.experimental.pallas import tpu as pltpu

F32 = jnp.float32
BF16 = jnp.bfloat16

D_MODEL = 1024
ATTN_HEADS = 6
HEAD_DIM = 64
ATTN_WIDTH = ATTN_HEADS * HEAD_DIM
MOBA_BLOCK = 256
MOBA_TOPK = 3
SSD_HEADS = 6
SSD_WIDTH = SSD_HEADS * HEAD_DIM
SSD_GROUPS = 2
SSD_STATE = 128
SSD_CONV = 4
SSD_CHUNK = 128
XBC_WIDTH = SSD_WIDTH + 2 * SSD_GROUPS * SSD_STATE
POOL_WINDOWS = (2, 4, 8, 16)
POOL_WIDTH = 256
MIX_WIDTH = 1024
FFN_HIDDEN = 2816
NORM_EPS = 1e-6
NEG_INF = -1e30

LANES = 128
SUBLANES = 8
QKV_W = 3 * ATTN_WIDTH
DT_PAD = LANES
XBC_UNITS = XBC_WIDTH // LANES
X_UNITS = SSD_WIDTH // LANES
QKV_UNITS = QKV_W // LANES
COL_Z = DT_PAD + XBC_WIDTH + QKV_W
COL_P = COL_Z + SSD_WIDTH
IN_PROJ_PAD = COL_P + POOL_WIDTH


def _mixed_unit(kind, u=0):
    if kind == "dt":
        return 0
    if kind == "x":
        return 1 + 2 * ((u - X_UNITS) % XBC_UNITS)
    return 2 + 2 * u if u < XBC_UNITS else 1 + XBC_UNITS + u


ROW_TILE = 512
SSD_UNROLL = 4
QK_LOOKAHEAD = 4
VMEM_LIMIT = 56 * 1024 * 1024


def _resident(shape, layer=None):
    nd = len(shape)
    if layer is None:
        return pl.BlockSpec(shape, lambda *_: (0,) * nd, pipeline_mode=pl.Buffered(1))
    return pl.BlockSpec((None,) + tuple(shape), lambda *_: (layer,) + (0,) * nd,
                        pipeline_mode=pl.Buffered(1))


def _rmsnorm(x, g):
    ms = jnp.mean(x * x, axis=-1, keepdims=True)
    return x * lax.rsqrt(ms + NORM_EPS) * g


def _softplus(x):
    return jnp.maximum(x, 0.0) + jnp.log1p(jnp.exp(-jnp.abs(x)))


def _silu(x):
    return x * jax.nn.sigmoid(x)


def _rows_back(x, tail, k):
    xw = jnp.concatenate([tail, x], axis=0)
    return pltpu.roll(xw, k, 0)[tail.shape[0]:, :]


def _expand_heads(cols):
    rows = cols.shape[0]
    lane = lax.broadcasted_iota(jnp.int32, (rows, LANES), 1)
    out = []
    for pair in range(SSD_HEADS // 2):
        lo = jnp.broadcast_to(cols[:, 2 * pair:2 * pair + 1], (rows, LANES))
        hi = jnp.broadcast_to(cols[:, 2 * pair + 1:2 * pair + 2], (rows, LANES))
        out.append(jnp.where(lane < HEAD_DIM, lo, hi))
    return jnp.concatenate(out, axis=1)


XH_SKIP = SSD_WIDTH
XH_B = 2 * SSD_WIDTH
XH_C = XH_B + SSD_GROUPS * SSD_STATE
XH_WIDTH = XH_C + SSD_GROUPS * SSD_STATE


W_IN_Z = QKV_W
W_IN_XBC = W_IN_Z + SSD_WIDTH
W_IN_DT = W_IN_XBC + XBC_WIDTH
W_IN_P = W_IN_DT + SSD_HEADS
W_IN_WIDTH = W_IN_P + POOL_WIDTH


def _in_proj_kernel(x_ref, g_ref, w_f_ref, w_tail_ref, cw_ref, cb_ref, dtb_ref, dskip_ref, pw_ref,
                    ps_ref, wo_f_ref, wgu_f_ref, wd_f_ref,
                    qkv_ref, zs_ref, xh_ref, dt_ref, yp_ref, wo_b_ref, wgu_b_ref, wd_b_ref,
                    ctail_ref, ptail_ref, w_ref, *, tiles_per_seq):
    @pl.when(pl.program_id(0) == 0)
    def _():
        w_ref[:, 0:DT_PAD] = w_tail_ref[:, 0:DT_PAD]
        for kind, lo, n in (("q", 0, QKV_UNITS), ("x", W_IN_XBC, XBC_UNITS)):
            for u in range(n):
                k = _mixed_unit(kind, u)
                w_ref[:, k * LANES:(k + 1) * LANES] = (
                    w_f_ref[:, lo + u * LANES:lo + (u + 1) * LANES].astype(BF16))
        w_ref[:, COL_Z:COL_P] = w_f_ref[:, W_IN_Z:W_IN_XBC].astype(BF16)
        w_ref[:, COL_P:IN_PROJ_PAD] = w_tail_ref[:, DT_PAD:]

    wo_b_ref[...] = wo_f_ref[...].astype(BF16)
    wgu_b_ref[...] = wgu_f_ref[...].astype(BF16)
    wd_b_ref[...] = wd_f_ref[...].astype(BF16)

    seq_tile = lax.rem(pl.program_id(0), tiles_per_seq)

    @pl.when(seq_tile == 0)
    def _():
        ctail_ref[...] = jnp.zeros(ctail_ref.shape, F32)
        ptail_ref[...] = jnp.zeros(ptail_ref.shape, F32)

    h = _rmsnorm(x_ref[...], g_ref[...]).astype(BF16)

    mix = jnp.dot(h, w_ref[:, 0:COL_Z], preferred_element_type=F32)
    p = jnp.dot(h, w_ref[:, COL_P:IN_PROJ_PAD], preferred_element_type=F32)
    z = jnp.dot(h, w_ref[:, COL_Z:COL_P], preferred_element_type=F32)

    def tile(kind, u=0):
        k = _mixed_unit(kind, u)
        return mix[:, k * LANES:(k + 1) * LANES]

    dt = _softplus(tile("dt") + dtb_ref[...])
    dt_ref[...] = dt
    dt_e = _expand_heads(dt)

    cw = cw_ref[...]
    n_ct = ctail_ref.shape[0]
    for u in sorted(range(XBC_UNITS), key=lambda v: _mixed_unit("x", v)):
        cols = slice(u * LANES, (u + 1) * LANES)
        xu = tile("x", u)
        conv = cb_ref[:, cols] + cw[SSD_CONV - 1:SSD_CONV, cols] * xu
        for j in range(SSD_CONV - 1):
            conv = conv + cw[j:j + 1, cols] * _rows_back(xu, ctail_ref[:, cols], SSD_CONV - 1 - j)
        ctail_ref[:, cols] = xu[ROW_TILE - n_ct:, :]
        act = _silu(conv)
        if u < X_UNITS:
            xh_ref[:, cols] = act * dt_e[:, cols]
            xh_ref[:, XH_SKIP + u * LANES:XH_SKIP + (u + 1) * LANES] = act * dskip_ref[:, cols]
        else:
            lo = XH_B + (u - X_UNITS) * LANES
            xh_ref[:, lo:lo + LANES] = act
    for u in range(QKV_UNITS):
        qkv_ref[:, u * LANES:(u + 1) * LANES] = tile("q", u)

    n_tail = ptail_ref.shape[0]
    win = jnp.concatenate([ptail_ref[...], p], axis=0)
    ptail_ref[...] = p[ROW_TILE - n_tail:, :]
    pos = seq_tile * ROW_TILE + lax.broadcasted_iota(jnp.int32, p.shape, 0)
    lane = lax.broadcasted_iota(jnp.int32, p.shape, 1)
    total = jnp.zeros(p.shape, F32)
    cnt = jnp.ones(p.shape, F32)
    width = 1
    for gi, w in enumerate(POOL_WINDOWS):
        while width < w:
            win = win + pltpu.roll(win, width, 0)
            width *= 2
        in_g = (lane >= gi * HEAD_DIM) & (lane < (gi + 1) * HEAD_DIM)
        total = jnp.where(in_g, win[n_tail:, :], total)
        cnt = jnp.where(in_g, jnp.minimum(pos + 1, w).astype(F32), cnt)
    d = total / cnt - p
    zs_ref[...] = _silu(z)
    yp_ref[...] = jnp.dot(d.astype(BF16), pw_ref[...], preferred_element_type=F32) * ps_ref[...]


def _slab_specs(rows, cols, steps, layer):
    bf16_rows = 2 * SUBLANES
    n_slabs = max(n for n in range(1, steps + 1)
                  if steps % n == 0 and rows % (n * bf16_rows) == 0)
    slab = rows // n_slabs
    per = steps // n_slabs
    src = pl.BlockSpec((None, slab, cols), lambda i: (layer, i // per, 0))
    dst = pl.BlockSpec((slab, cols), lambda i: (i // per, 0))
    return src, dst


def _in_proj(x2, g, w, w_tail, cw, cb, dtb, dskip, pw, ps, wo, wgu, wd, seq, layer):
    m = x2.shape[0]
    steps = m // ROW_TILE
    outs = ((QKV_W, F32), (SSD_WIDTH, F32), (XH_WIDTH, F32), (DT_PAD, F32), (POOL_WIDTH, F32))
    slabs = [_slab_specs(a.shape[1], a.shape[2], steps, layer) for a in (wo, wgu, wd)]
    kern = functools.partial(_in_proj_kernel, tiles_per_seq=seq // ROW_TILE)
    return pl.pallas_call(
        kern,
        grid=(steps,),
        in_specs=[pl.BlockSpec((ROW_TILE, D_MODEL), lambda i: (i, 0)),
                  _resident((1, D_MODEL), layer),
                  _resident((D_MODEL, W_IN_WIDTH), layer),
                  _resident((D_MODEL, DT_PAD + POOL_WIDTH), layer),
                  _resident((SSD_CONV, XBC_WIDTH), layer), _resident((1, XBC_WIDTH), layer),
                  _resident((1, DT_PAD), layer), _resident((1, SSD_WIDTH), layer),
                  _resident((POOL_WIDTH, POOL_WIDTH), layer), _resident((1, POOL_WIDTH), layer)]
        + [src for src, _ in slabs],
        out_specs=[pl.BlockSpec((ROW_TILE, wd_), lambda i: (i, 0)) for wd_, _ in outs]
        + [dst for _, dst in slabs],
        out_shape=[jax.ShapeDtypeStruct((m, wd_), dt) for wd_, dt in outs]
        + [jax.ShapeDtypeStruct(a.shape[1:], BF16) for a in (wo, wgu, wd)],
        scratch_shapes=[pltpu.VMEM((SUBLANES, XBC_WIDTH), F32),
                        pltpu.VMEM((2 * SUBLANES, POOL_WIDTH), F32),
                        pltpu.VMEM((D_MODEL, IN_PROJ_PAD), BF16)],
        compiler_params=pltpu.CompilerParams(dimension_semantics=("arbitrary",),
                                             vmem_limit_bytes=VMEM_LIMIT),
        name="in_proj",
    )(x2, g, w, w_tail, cw, cb, dtb, dskip, pw, ps, wo, wgu, wd)


def _moba_bias(gate_t, blk):
    nb = gate_t.shape[0]
    blk_id = lax.broadcasted_iota(jnp.int32, gate_t.shape, 0)
    valid = blk_id < blk
    g = jnp.where(valid, gate_t, NEG_INF)
    rank = jnp.zeros(gate_t.shape, jnp.int32)
    for m in range(nb):
        gm = g[m:m + 1, :]
        beats = (gm > g) | ((gm == g) & (m < blk_id))
        rank = rank + beats.astype(jnp.int32)
    sel = valid & (rank < MOBA_TOPK)
    return jnp.where(sel, 0.0, NEG_INF).astype(F32)


def _attn_kernel(q_ref, k_ref, v_ref, o_ref, kb_ref, vta_ref, vtb_ref, kmean_ref, *, seq):
    n_blocks = seq // MOBA_BLOCK
    tq = MOBA_BLOCK
    k = k_ref[...]
    kmean_ref[...] = jnp.mean(k.reshape(n_blocks, MOBA_BLOCK, LANES), axis=1)
    kb_ref[...] = k.astype(BF16)
    vt = v_ref[...].T
    row = lax.broadcasted_iota(jnp.int32, vt.shape, 0)
    vta_ref[...] = jnp.where(row < HEAD_DIM, vt, jnp.where(row == HEAD_DIM, 1.0, 0.0)).astype(BF16)
    vtb_ref[...] = jnp.where(row >= HEAD_DIM, vt, jnp.where(row == 0, 1.0, 0.0)).astype(BF16)

    lane = lax.broadcasted_iota(jnp.int32, (tq, LANES), 1)
    half = tq // 2
    tri = (lax.broadcasted_iota(jnp.int32, (half, half), 0)
           <= lax.broadcasted_iota(jnp.int32, (half, half), 1))
    out_row = lax.broadcasted_iota(jnp.int32, (LANES, tq), 0)
    nt = (((1,), (1,)), ((), ()))
    scale2 = HEAD_DIM ** -0.5 * math.log2(math.e)

    class Tile:
        def __init__(self, i, head):
            self.i, self.head = i, head
            self.vt_ref = vta_ref if head == 0 else vtb_ref
            self.blocks, self.m, self.acc = [], None, None

        def score_step(self, n):
            i = self.i
            if n == 0:
                q = q_ref[i * tq:(i + 1) * tq, :]
                in_head = (lane < HEAD_DIM) if self.head == 0 else (lane >= HEAD_DIM)
                qh = jnp.where(in_head, q, 0.0)
                self.qs = (qh * scale2).astype(BF16)
                self.bias = None
                if i > MOBA_TOPK:
                    gate_t = lax.dot_general(kmean_ref[...], qh, nt, preferred_element_type=F32,
                                             precision=lax.Precision.HIGHEST)
                    self.bias = _moba_bias(gate_t, i)
            s = lax.dot_general(kb_ref[n * tq:(n + 1) * tq, :], self.qs, nt,
                                preferred_element_type=F32)
            if n == i:
                top_l = jnp.where(tri, s[:half, :half], NEG_INF)
                top_r = s[:half, half:]
                bot_r = jnp.where(tri, s[half:, half:], NEG_INF)
                bm = jnp.concatenate(
                    [jnp.max(top_l, axis=0, keepdims=True),
                     jnp.maximum(jnp.max(top_r, axis=0, keepdims=True),
                                 jnp.max(bot_r, axis=0, keepdims=True))], axis=1)
                self.blocks.append((top_l, top_r, bot_r))
            else:
                bm = jnp.max(s, axis=0, keepdims=True)
                if self.bias is not None:
                    bm = bm + self.bias[n:n + 1, :]
                self.blocks.append(s)
            self.m = bm if self.m is None else jnp.maximum(self.m, bm)

        def pv_step(self, n):
            m = self.m
            if n == self.i:
                top_l, top_r, bot_r = self.blocks[n]
                m_l, m_r = m[:, :half], m[:, half:]
                p_bot_r = jnp.exp2(bot_r - m_r)
                p = jnp.concatenate(
                    [jnp.concatenate([jnp.exp2(top_l - m_l), jnp.exp2(top_r - m_r)], axis=1),
                     jnp.concatenate([jnp.zeros_like(p_bot_r), p_bot_r], axis=1)], axis=0)
            else:
                if self.bias is not None:
                    m = m - self.bias[n:n + 1, :]
                p = jnp.exp2(self.blocks[n] - m)
            part = jnp.dot(self.vt_ref[:, n * tq:(n + 1) * tq], p.astype(BF16),
                           preferred_element_type=F32)
            self.acc = part if self.acc is None else self.acc + part

    tiles = [Tile(i, head) for i in range(n_blocks) for head in (0, 1)]
    for t in tiles[:QK_LOOKAHEAD]:
        for n in range(t.i + 1):
            t.score_step(n)
    done = []
    for j, t in enumerate(tiles):
        ahead = tiles[j + QK_LOOKAHEAD] if j + QK_LOOKAHEAD < len(tiles) else None
        n_ahead = ahead.i + 1 if ahead is not None else 0
        for n in range(max(t.i + 1, n_ahead)):
            if n < n_ahead:
                ahead.score_step(n)
            if n <= t.i:
                t.pv_step(n)
        done.append(t.acc)
        if t.head == 1:
            out_t = jnp.where(out_row < HEAD_DIM, done[0] / done[0][HEAD_DIM:HEAD_DIM + 1, :],
                              done[1] / done[1][0:1, :])
            o_ref[t.i * tq:(t.i + 1) * tq, :] = out_t.T
            done = []


def _attention(qkv3):
    b, seq, _ = qkv3.shape
    n_pairs = ATTN_WIDTH // LANES
    kern = functools.partial(_attn_kernel, seq=seq)
    return pl.pallas_call(
        kern,
        grid=(b, n_pairs),
        in_specs=[pl.BlockSpec((None, seq, LANES), lambda bi, j: (bi, 0, j)),
                  pl.BlockSpec((None, seq, LANES), lambda bi, j: (bi, 0, n_pairs + j)),
                  pl.BlockSpec((None, seq, LANES), lambda bi, j: (bi, 0, 2 * n_pairs + j))],
        out_specs=pl.BlockSpec((None, seq, LANES), lambda bi, j: (bi, 0, j)),
        out_shape=jax.ShapeDtypeStruct((b, seq, ATTN_WIDTH), F32),
        scratch_shapes=[pltpu.VMEM((seq, LANES), BF16),
                        pltpu.VMEM((LANES, seq), BF16),
                        pltpu.VMEM((LANES, seq), BF16),
                        pltpu.VMEM((seq // MOBA_BLOCK, LANES), F32)],
        compiler_params=pltpu.CompilerParams(dimension_semantics=("arbitrary", "arbitrary"),
                                             vmem_limit_bytes=VMEM_LIMIT),
        name="moba_attn",
    )(qkv3, qkv3, qkv3)


def _cumsum_rows(x):
    n = x.shape[0]
    row = lax.broadcasted_iota(jnp.int32, x.shape, 0)
    k = 1
    while k < n:
        x = x + jnp.where(row >= k, pltpu.roll(x, k, 0), 0.0)
        k *= 2
    return x


def _ssd_kernel(xh_ref, zs_ref, dt_ref, alog_ref, nw_ref, o_ref, state_ref, *, seq):
    L = SSD_CHUNK
    n_chunks = seq // L
    gw = SSD_WIDTH // SSD_GROUPS
    per_group = SSD_HEADS // SSD_GROUPS
    state_ref[...] = jnp.zeros(state_ref.shape, F32)

    r_i = lax.broadcasted_iota(jnp.int32, (L, L), 0)
    c_i = lax.broadcasted_iota(jnp.int32, (L, L), 1)
    tril = r_i >= c_i
    lane_w = lax.broadcasted_iota(jnp.int32, (L, SSD_WIDTH), 1)
    low_half = lax.broadcasted_iota(jnp.int32, (L, LANES), 1) < HEAD_DIM
    a_neg = -jnp.exp(alog_ref[...])
    nt = (((1,), (1,)), ((), ()))

    def chunk(c, carry):
        r0 = pl.multiple_of(c * L, L)
        b_in = xh_ref[pl.ds(r0, L), XH_B:XH_C]
        c_in = xh_ref[pl.ds(r0, L), XH_C:XH_WIDTH]

        xdt = xh_ref[pl.ds(r0, L), 0:XH_SKIP]
        dt = dt_ref[pl.ds(r0, L), :]
        acum = _cumsum_rows(dt * a_neg)
        acum_t = acum.T
        acum_b = [jnp.broadcast_to(acum[:, h:h + 1], (L, LANES)) for h in range(SSD_HEADS)]
        acum_e = jnp.concatenate([jnp.where(low_half, acum_b[2 * pr], acum_b[2 * pr + 1])
                                  for pr in range(SSD_HEADS // 2)], axis=1)
        last_e = acum_e[L - 1:L, :]
        xdt_end = xdt * jnp.exp(last_e - acum_e)

        state = state_ref[...]
        new_state = state * jnp.exp(last_e)
        y_off = jnp.zeros((L, SSD_WIDTH), F32)
        cb = []
        for g in range(SSD_GROUPS):
            in_g = (lane_w >= g * gw) & (lane_w < (g + 1) * gw)
            bg = b_in[:, g * SSD_STATE:(g + 1) * SSD_STATE]
            cg = c_in[:, g * SSD_STATE:(g + 1) * SSD_STATE].astype(BF16)
            cb.append(lax.dot_general(cg, bg.astype(BF16), nt, preferred_element_type=F32))
            y_off = y_off + jnp.dot(cg, jnp.where(in_g[:SSD_STATE], state, 0.0).astype(BF16),
                                    preferred_element_type=F32)
            new_state = new_state + jnp.dot(bg.T.astype(BF16),
                                            jnp.where(in_g, xdt_end, 0.0).astype(BF16),
                                            preferred_element_type=F32)
        state_ref[...] = new_state

        y_diag = []
        for pr in range(SSD_HEADS // 2):
            xp = xdt[:, pr * LANES:(pr + 1) * LANES].astype(BF16)
            halves = []
            for h in (2 * pr, 2 * pr + 1):
                decay = jnp.exp(jnp.where(tril, acum_b[h] - acum_t[h:h + 1, :], NEG_INF))
                halves.append(jnp.dot((cb[h // per_group] * decay).astype(BF16), xp,
                                      preferred_element_type=F32))
            y_diag.append(jnp.where(low_half, halves[0], halves[1]))
        y = (jnp.concatenate(y_diag, axis=1) + xh_ref[pl.ds(r0, L), XH_SKIP:XH_B]
             + y_off * jnp.exp(acum_e))

        y = y * zs_ref[pl.ds(r0, L), :]
        y2 = y * y
        inv = jnp.zeros((L, SSD_WIDTH), F32)
        for g in range(SSD_GROUPS):
            in_g = (lane_w >= g * gw) & (lane_w < (g + 1) * gw)
            ms = jnp.sum(jnp.where(in_g, y2, 0.0), axis=-1, keepdims=True) * (1.0 / gw)
            inv = jnp.where(in_g, lax.rsqrt(ms + NORM_EPS), inv)
        o_ref[pl.ds(r0, L), :] = y * inv * nw_ref[...]
        return carry

    lax.fori_loop(0, n_chunks, chunk, 0, unroll=SSD_UNROLL)


def _ssd(xh3, zs3, dt3, alog, nw, layer):
    b, seq, _ = xh3.shape
    kern = functools.partial(_ssd_kernel, seq=seq)
    per_batch = lambda wd: pl.BlockSpec((None, seq, wd), lambda bi: (bi, 0, 0))
    return pl.pallas_call(
        kern,
        grid=(b,),
        in_specs=[per_batch(XH_WIDTH), per_batch(SSD_WIDTH), per_batch(DT_PAD),
                  _resident((1, DT_PAD), layer), _resident((1, SSD_WIDTH), layer)],
        out_specs=per_batch(SSD_WIDTH),
        out_shape=jax.ShapeDtypeStruct((b, seq, SSD_WIDTH), F32),
        scratch_shapes=[pltpu.VMEM((SSD_STATE, SSD_WIDTH), F32)],
        compiler_params=pltpu.CompilerParams(dimension_semantics=("arbitrary",),
                                             vmem_limit_bytes=VMEM_LIMIT),
        name="ssd_mixer",
    )(xh3, zs3, dt3, alog, nw)


def _out_ffn_kernel(x_ref, ya_ref, ys_ref, yp_ref, wo_ref, g_ref, wgu_ref, wd_ref, gf_ref, o_ref,
                    *, final):
    y = jnp.concatenate([ya_ref[...], ys_ref[...], yp_ref[...]], axis=1).astype(BF16)
    x = x_ref[...] + jnp.dot(y, wo_ref[...], preferred_element_type=F32)
    h = _rmsnorm(x, g_ref[...]).astype(BF16)
    gate = jnp.dot(h, wgu_ref[:, :FFN_HIDDEN], preferred_element_type=F32)
    up = jnp.dot(h, wgu_ref[:, FFN_HIDDEN:], preferred_element_type=F32)
    a = (_silu(gate) * up).astype(BF16)
    acc = x + jnp.dot(a, wd_ref[...], preferred_element_type=F32)
    if final:
        acc = _rmsnorm(acc, gf_ref[...])
    o_ref[...] = acc


def _out_ffn(x2, ya, ys, yp, wo, g, wgu, wd, gf, final, layer):
    m = x2.shape[0]
    kern = functools.partial(_out_ffn_kernel, final=final)
    rows = lambda wd_: pl.BlockSpec((ROW_TILE, wd_), lambda i: (i, 0))
    return pl.pallas_call(
        kern,
        grid=(m // ROW_TILE,),
        in_specs=[rows(D_MODEL), rows(ATTN_WIDTH), rows(SSD_WIDTH), rows(POOL_WIDTH),
                  _resident((MIX_WIDTH, D_MODEL)), _resident((1, D_MODEL), layer),
                  _resident((D_MODEL, 2 * FFN_HIDDEN)), _resident((FFN_HIDDEN, D_MODEL)),
                  _resident((1, D_MODEL))],
        out_specs=rows(D_MODEL),
        out_shape=jax.ShapeDtypeStruct((m, D_MODEL), F32),
        compiler_params=pltpu.CompilerParams(dimension_semantics=("arbitrary",),
                                             vmem_limit_bytes=VMEM_LIMIT),
        name="out_ffn",
    )(x2, ya, ys, yp, wo, g, wgu, wd, gf)


def _pad_lanes(v, width):
    return jnp.pad(v, [(0, 0)] * (v.ndim - 1) + [(0, width - v.shape[-1])])


def kernel(x, norm_mix, w_in, conv_w, conv_b, dt_bias, a_log, d_skip, ssd_norm, pool_w, pool_scale,
           w_out, norm_ffn, w_gate_up, w_down, norm_final):
    bsz, seq, d = x.shape
    depth = w_in.shape[0]
    m = bsz * seq
    w_tail = jnp.concatenate([_pad_lanes(w_in[:, :, W_IN_DT:W_IN_P], DT_PAD), w_in[:, :, W_IN_P:]],
                             axis=-1).astype(BF16)
    eye = jnp.eye(len(POOL_WINDOWS), dtype=F32)
    pool_bd = (eye[None, :, None, :, None] * pool_w[:, :, :, None, :]).reshape(
        depth, POOL_WIDTH, POOL_WIDTH).astype(BF16)
    row = lambda v: v.reshape(depth, 1, v.shape[-1])
    dtb_p = row(_pad_lanes(dt_bias, DT_PAD))
    alog_p = row(_pad_lanes(a_log, DT_PAD))
    dskip_e = row(jnp.repeat(d_skip, HEAD_DIM, axis=-1))
    g_mix, g_ffn, conv_b3 = row(norm_mix), row(norm_ffn), row(conv_b)
    pool_s3, ssd_norm3 = row(pool_scale), row(ssd_norm)

    x2 = x.reshape(m, d)
    gf = norm_final.reshape(1, d)
    for l in range(depth):
        qkv, zs, xh, dt, yp, wo_b, wgu_b, wd_b = _in_proj(
            x2, g_mix, w_in, w_tail, conv_w, conv_b3, dtb_p, dskip_e, pool_bd, pool_s3,
            w_out, w_gate_up, w_down, seq, l)
        ya = _attention(qkv.reshape(bsz, seq, QKV_W))
        ys = _ssd(xh.reshape(bsz, seq, XH_WIDTH), zs.reshape(bsz, seq, SSD_WIDTH),
                  dt.reshape(bsz, seq, DT_PAD), alog_p, ssd_norm3, l)
        x2 = _out_ffn(x2, ya.reshape(m, ATTN_WIDTH), ys.reshape(m, SSD_WIDTH), yp,
                      wo_b, g_ffn, wgu_b, wd_b, gf, final=(l == depth - 1), layer=l)
    return x2.reshape(bsz, seq, d)
```

```python
import functools
import math

import jax
import jax.numpy as jnp
from jax import lax
from jax.experimental import pallas as pl
from jax.experimental.pallas import tpu as pltpu

F32 = jnp.float32
BF16 = jnp.bfloat16

D_MODEL = 1024
ATTN_HEADS = 6
HEAD_DIM = 64
ATTN_WIDTH = ATTN_HEADS * HEAD_DIM
MOBA_BLOCK = 256
MOBA_TOPK = 3
SSD_HEADS = 6
SSD_WIDTH = SSD_HEADS * HEAD_DIM
SSD_GROUPS = 2
SSD_STATE = 128
SSD_CONV = 4
SSD_CHUNK = 128
XBC_WIDTH = SSD_WIDTH + 2 * SSD_GROUPS * SSD_STATE
POOL_WINDOWS = (2, 4, 8, 16)
POOL_WIDTH = 256
MIX_WIDTH = 1024
FFN_HIDDEN = 2816
NORM_EPS = 1e-6
NEG_INF = -1e30

LANES = 128
SUBLANES = 8
QKV_W = 3 * ATTN_WIDTH
DT_PAD = LANES
XBC_UNITS = XBC_WIDTH // LANES
X_UNITS = SSD_WIDTH // LANES
QKV_UNITS = QKV_W // LANES
COL_Z = DT_PAD + XBC_WIDTH + QKV_W
COL_P = COL_Z + SSD_WIDTH
IN_PROJ_PAD = COL_P + POOL_WIDTH


def _mixed_unit(kind, u=0):
    if kind == "dt":
        return 0
    if kind == "x":
        return 1 + 2 * ((u - X_UNITS) % XBC_UNITS)
    return 2 + 2 * u if u < XBC_UNITS else 1 + XBC_UNITS + u


ROW_TILE = 512
SSD_UNROLL = 4
QK_LOOKAHEAD = 4
VMEM_LIMIT = 56 * 1024 * 1024


def _resident(shape, layer=None):
    nd = len(shape)
    if layer is None:
        return pl.BlockSpec(shape, lambda *_: (0,) * nd, pipeline_mode=pl.Buffered(1))
    return pl.BlockSpec((None,) + tuple(shape), lambda *_: (layer,) + (0,) * nd,
                        pipeline_mode=pl.Buffered(1))


def _rmsnorm(x, g):
    ms = jnp.mean(x * x, axis=-1, keepdims=True)
    return x * lax.rsqrt(ms + NORM_EPS) * g


def _softplus(x):
    return jnp.maximum(x, 0.0) + jnp.log1p(jnp.exp(-jnp.abs(x)))


def _silu(x):
    return x * jax.nn.sigmoid(x)


def _rows_back(x, tail, k):
    xw = jnp.concatenate([tail, x], axis=0)
    return pltpu.roll(xw, k, 0)[tail.shape[0]:, :]


def _expand_heads(cols):
    rows = cols.shape[0]
    lane = lax.broadcasted_iota(jnp.int32, (rows, LANES), 1)
    out = []
    for pair in range(SSD_HEADS // 2):
        lo = jnp.broadcast_to(cols[:, 2 * pair:2 * pair + 1], (rows, LANES))
        hi = jnp.broadcast_to(cols[:, 2 * pair + 1:2 * pair + 2], (rows, LANES))
        out.append(jnp.where(lane < HEAD_DIM, lo, hi))
    return jnp.concatenate(out, axis=1)


XH_SKIP = SSD_WIDTH
XH_WIDTH = 2 * SSD_WIDTH
BC_WIDTH = 2 * SSD_GROUPS * SSD_STATE


W_IN_Z = QKV_W
W_IN_XBC = W_IN_Z + SSD_WIDTH
W_IN_DT = W_IN_XBC + XBC_WIDTH
W_IN_P = W_IN_DT + SSD_HEADS
W_IN_WIDTH = W_IN_P + POOL_WIDTH


def _in_proj_kernel(x_ref, g_ref, w_ref, cw_ref, cb_ref, dtb_ref, dskip_ref, pw_ref, ps_ref,
                    wo_f_ref, wgu_f_ref, wd_f_ref,
                    qk_ref, v_ref, zs_ref, xh_ref, bc_ref, dt_ref, yp_ref,
                    wo_b_ref, wgu_b_ref, wd_b_ref, ctail_ref, ptail_ref, *, tiles_per_seq):
    wo_b_ref[...] = wo_f_ref[...].astype(BF16)
    wgu_b_ref[...] = wgu_f_ref[...].astype(BF16)
    wd_b_ref[...] = wd_f_ref[...].astype(BF16)

    seq_tile = lax.rem(pl.program_id(0), tiles_per_seq)

    @pl.when(seq_tile == 0)
    def _():
        ctail_ref[...] = jnp.zeros(ctail_ref.shape, F32)
        ptail_ref[...] = jnp.zeros(ptail_ref.shape, F32)

    h = _rmsnorm(x_ref[...], g_ref[...]).astype(BF16)

    mix = jnp.dot(h, w_ref[:, 0:COL_Z], preferred_element_type=F32)
    p = jnp.dot(h, w_ref[:, COL_P:IN_PROJ_PAD], preferred_element_type=F32)
    z = jnp.dot(h, w_ref[:, COL_Z:COL_P], preferred_element_type=F32)

    def tile(kind, u=0):
        k = _mixed_unit(kind, u)
        return mix[:, k * LANES:(k + 1) * LANES]

    dt = _softplus(tile("dt") + dtb_ref[...])
    dt_ref[...] = dt
    dt_e = _expand_heads(dt)

    cw = cw_ref[...]
    n_ct = ctail_ref.shape[0]
    for u in sorted(range(XBC_UNITS), key=lambda v: _mixed_unit("x", v)):
        cols = slice(u * LANES, (u + 1) * LANES)
        xu = tile("x", u)
        conv = cb_ref[:, cols] + cw[SSD_CONV - 1:SSD_CONV, cols] * xu
        for j in range(SSD_CONV - 1):
            conv = conv + cw[j:j + 1, cols] * _rows_back(xu, ctail_ref[:, cols], SSD_CONV - 1 - j)
        ctail_ref[:, cols] = xu[ROW_TILE - n_ct:, :]
        act = _silu(conv)
        if u < X_UNITS:
            xh_ref[:, cols] = act * dt_e[:, cols]
            xh_ref[:, XH_SKIP + u * LANES:XH_SKIP + (u + 1) * LANES] = act * dskip_ref[:, cols]
        else:
            lo = (u - X_UNITS) * LANES
            bc_ref[:, lo:lo + LANES] = act.astype(BF16)
    qk_units = 2 * ATTN_WIDTH // LANES
    for u in range(QKV_UNITS):
        if u < qk_units:
            qk_ref[:, u * LANES:(u + 1) * LANES] = tile("q", u)
        else:
            v_ref[:, (u - qk_units) * LANES:(u - qk_units + 1) * LANES] = tile("q", u).astype(BF16)

    n_tail = ptail_ref.shape[0]
    win = jnp.concatenate([ptail_ref[...], p], axis=0)
    ptail_ref[...] = p[ROW_TILE - n_tail:, :]
    pos = seq_tile * ROW_TILE + lax.broadcasted_iota(jnp.int32, p.shape, 0)
    lane = lax.broadcasted_iota(jnp.int32, p.shape, 1)
    total = jnp.zeros(p.shape, F32)
    cnt = jnp.ones(p.shape, F32)
    width = 1
    for gi, w in enumerate(POOL_WINDOWS):
        while width < w:
            win = win + pltpu.roll(win, width, 0)
            width *= 2
        in_g = (lane >= gi * HEAD_DIM) & (lane < (gi + 1) * HEAD_DIM)
        total = jnp.where(in_g, win[n_tail:, :], total)
        cnt = jnp.where(in_g, jnp.minimum(pos + 1, w).astype(F32), cnt)
    d = total / cnt - p
    zs_ref[...] = _silu(z)
    yp_ref[...] = jnp.dot(d.astype(BF16), pw_ref[...], preferred_element_type=F32) * ps_ref[...]


def _slab_specs(rows, cols, steps, layer):
    bf16_rows = 2 * SUBLANES
    n_slabs = max(n for n in range(1, steps + 1)
                  if steps % n == 0 and rows % (n * bf16_rows) == 0)
    slab = rows // n_slabs
    per = steps // n_slabs
    src = pl.BlockSpec((None, slab, cols), lambda i: (layer, i // per, 0))
    dst = pl.BlockSpec((slab, cols), lambda i: (i // per, 0))
    return src, dst


def _in_proj(x2, g, w, cw, cb, dtb, dskip, pw, ps, wo, wgu, wd, seq, layer):
    m = x2.shape[0]
    steps = m // ROW_TILE
    outs = ((2 * ATTN_WIDTH, F32), (ATTN_WIDTH, BF16), (SSD_WIDTH, F32), (XH_WIDTH, F32),
            (BC_WIDTH, BF16), (DT_PAD, F32), (POOL_WIDTH, F32))
    slabs = [_slab_specs(a.shape[1], a.shape[2], steps, layer) for a in (wo, wgu, wd)]
    kern = functools.partial(_in_proj_kernel, tiles_per_seq=seq // ROW_TILE)
    return pl.pallas_call(
        kern,
        grid=(steps,),
        in_specs=[pl.BlockSpec((ROW_TILE, D_MODEL), lambda i: (i, 0)),
                  _resident((1, D_MODEL), layer),
                  _resident((D_MODEL, IN_PROJ_PAD), layer),
                  _resident((SSD_CONV, XBC_WIDTH), layer), _resident((1, XBC_WIDTH), layer),
                  _resident((1, DT_PAD), layer), _resident((1, SSD_WIDTH), layer),
                  _resident((POOL_WIDTH, POOL_WIDTH), layer), _resident((1, POOL_WIDTH), layer)]
        + [src for src, _ in slabs],
        out_specs=[pl.BlockSpec((ROW_TILE, wd_), lambda i: (i, 0)) for wd_, _ in outs]
        + [dst for _, dst in slabs],
        out_shape=[jax.ShapeDtypeStruct((m, wd_), dt) for wd_, dt in outs]
        + [jax.ShapeDtypeStruct(a.shape[1:], BF16) for a in (wo, wgu, wd)],
        scratch_shapes=[pltpu.VMEM((SUBLANES, XBC_WIDTH), F32),
                        pltpu.VMEM((2 * SUBLANES, POOL_WIDTH), F32)],
        compiler_params=pltpu.CompilerParams(dimension_semantics=("arbitrary",),
                                             vmem_limit_bytes=VMEM_LIMIT),
        name="in_proj",
    )(x2, g, w, cw, cb, dtb, dskip, pw, ps, wo, wgu, wd)


def _moba_bias(gate_t, blk):
    nb = gate_t.shape[0]
    blk_id = lax.broadcasted_iota(jnp.int32, gate_t.shape, 0)
    valid = blk_id < blk
    g = jnp.where(valid, gate_t, NEG_INF)
    rank = jnp.zeros(gate_t.shape, jnp.int32)
    for m in range(nb):
        gm = g[m:m + 1, :]
        beats = (gm > g) | ((gm == g) & (m < blk_id))
        rank = rank + beats.astype(jnp.int32)
    sel = valid & (rank < MOBA_TOPK)
    return jnp.where(sel, 0.0, NEG_INF).astype(F32)


def _attn_kernel(q_ref, k_ref, v_ref, o_ref, kb_ref, vta_ref, vtb_ref, kmean_ref, *, seq):
    n_blocks = seq // MOBA_BLOCK
    tq = MOBA_BLOCK
    k = k_ref[...]
    kmean_ref[...] = jnp.mean(k.reshape(n_blocks, MOBA_BLOCK, LANES), axis=1)
    kb_ref[...] = k.astype(BF16)
    vt = v_ref[...].astype(F32).T
    row = lax.broadcasted_iota(jnp.int32, vt.shape, 0)
    vta_ref[...] = jnp.where(row < HEAD_DIM, vt, jnp.where(row == HEAD_DIM, 1.0, 0.0)).astype(BF16)
    vtb_ref[...] = jnp.where(row >= HEAD_DIM, vt, jnp.where(row == 0, 1.0, 0.0)).astype(BF16)

    lane = lax.broadcasted_iota(jnp.int32, (tq, LANES), 1)
    half = tq // 2
    tri = (lax.broadcasted_iota(jnp.int32, (half, half), 0)
           <= lax.broadcasted_iota(jnp.int32, (half, half), 1))
    out_row = lax.broadcasted_iota(jnp.int32, (LANES, tq), 0)
    nt = (((1,), (1,)), ((), ()))
    scale2 = HEAD_DIM ** -0.5 * math.log2(math.e)

    class Tile:
        def __init__(self, i, head):
            self.i, self.head = i, head
            self.vt_ref = vta_ref if head == 0 else vtb_ref
            self.blocks, self.m, self.acc = [], None, None

        def score_step(self, n):
            i = self.i
            if n == 0:
                q = q_ref[i * tq:(i + 1) * tq, :]
                in_head = (lane < HEAD_DIM) if self.head == 0 else (lane >= HEAD_DIM)
                qh = jnp.where(in_head, q, 0.0)
                self.qs = (qh * scale2).astype(BF16)
                self.bias = None
                if i > MOBA_TOPK:
                    gate_t = lax.dot_general(kmean_ref[...], qh, nt, preferred_element_type=F32,
                                             precision=lax.Precision.HIGHEST)
                    self.bias = _moba_bias(gate_t, i)
            s = lax.dot_general(kb_ref[n * tq:(n + 1) * tq, :], self.qs, nt,
                                preferred_element_type=F32)
            if n == i:
                top_l = jnp.where(tri, s[:half, :half], NEG_INF)
                top_r = s[:half, half:]
                bot_r = jnp.where(tri, s[half:, half:], NEG_INF)
                bm = jnp.concatenate(
                    [jnp.max(top_l, axis=0, keepdims=True),
                     jnp.maximum(jnp.max(top_r, axis=0, keepdims=True),
                                 jnp.max(bot_r, axis=0, keepdims=True))], axis=1)
                self.blocks.append((top_l, top_r, bot_r))
            else:
                bm = jnp.max(s, axis=0, keepdims=True)
                if self.bias is not None:
                    bm = bm + self.bias[n:n + 1, :]
                self.blocks.append(s)
            self.m = bm if self.m is None else jnp.maximum(self.m, bm)

        def pv_step(self, n):
            m = self.m
            if n == self.i:
                top_l, top_r, bot_r = self.blocks[n]
                m_l, m_r = m[:, :half], m[:, half:]
                p_bot_r = jnp.exp2(bot_r - m_r)
                p = jnp.concatenate(
                    [jnp.concatenate([jnp.exp2(top_l - m_l), jnp.exp2(top_r - m_r)], axis=1),
                     jnp.concatenate([jnp.zeros_like(p_bot_r), p_bot_r], axis=1)], axis=0)
            else:
                if self.bias is not None:
                    m = m - self.bias[n:n + 1, :]
                p = jnp.exp2(self.blocks[n] - m)
            part = jnp.dot(self.vt_ref[:, n * tq:(n + 1) * tq], p.astype(BF16),
                           preferred_element_type=F32)
            self.acc = part if self.acc is None else self.acc + part

    tiles = [Tile(i, head) for i in range(n_blocks) for head in (0, 1)]
    for t in tiles[:QK_LOOKAHEAD]:
        for n in range(t.i + 1):
            t.score_step(n)
    done = []
    for j, t in enumerate(tiles):
        ahead = tiles[j + QK_LOOKAHEAD] if j + QK_LOOKAHEAD < len(tiles) else None
        n_ahead = ahead.i + 1 if ahead is not None else 0
        for n in range(max(t.i + 1, n_ahead)):
            if n < n_ahead:
                ahead.score_step(n)
            if n <= t.i:
                t.pv_step(n)
        done.append(t.acc)
        if t.head == 1:
            out_t = jnp.where(out_row < HEAD_DIM, done[0] / done[0][HEAD_DIM:HEAD_DIM + 1, :],
                              done[1] / done[1][0:1, :])
            o_ref[t.i * tq:(t.i + 1) * tq, :] = out_t.T
            done = []


def _attention(qk3, v3):
    b, seq, _ = qk3.shape
    n_pairs = ATTN_WIDTH // LANES
    kern = functools.partial(_attn_kernel, seq=seq)
    return pl.pallas_call(
        kern,
        grid=(b, n_pairs),
        in_specs=[pl.BlockSpec((None, seq, LANES), lambda bi, j: (bi, 0, j)),
                  pl.BlockSpec((None, seq, LANES), lambda bi, j: (bi, 0, n_pairs + j)),
                  pl.BlockSpec((None, seq, LANES), lambda bi, j: (bi, 0, j))],
        out_specs=pl.BlockSpec((None, seq, LANES), lambda bi, j: (bi, 0, j)),
        out_shape=jax.ShapeDtypeStruct((b, seq, ATTN_WIDTH), F32),
        scratch_shapes=[pltpu.VMEM((seq, LANES), BF16),
                        pltpu.VMEM((LANES, seq), BF16),
                        pltpu.VMEM((LANES, seq), BF16),
                        pltpu.VMEM((seq // MOBA_BLOCK, LANES), F32)],
        compiler_params=pltpu.CompilerParams(dimension_semantics=("arbitrary", "arbitrary"),
                                             vmem_limit_bytes=VMEM_LIMIT),
        name="moba_attn",
    )(qk3, qk3, v3)


def _cumsum_rows(x):
    n = x.shape[0]
    row = lax.broadcasted_iota(jnp.int32, x.shape, 0)
    k = 1
    while k < n:
        x = x + jnp.where(row >= k, pltpu.roll(x, k, 0), 0.0)
        k *= 2
    return x


def _ssd_kernel(xh_ref, bc_ref, zs_ref, dt_ref, alog_ref, nw_ref, o_ref, state_ref, *, seq):
    L = SSD_CHUNK
    n_chunks = seq // L
    gw = SSD_WIDTH // SSD_GROUPS
    per_group = SSD_HEADS // SSD_GROUPS
    state_ref[...] = jnp.zeros(state_ref.shape, F32)

    r_i = lax.broadcasted_iota(jnp.int32, (L, L), 0)
    c_i = lax.broadcasted_iota(jnp.int32, (L, L), 1)
    tril = r_i >= c_i
    lane_w = lax.broadcasted_iota(jnp.int32, (L, SSD_WIDTH), 1)
    low_half = lax.broadcasted_iota(jnp.int32, (L, LANES), 1) < HEAD_DIM
    a_neg = -jnp.exp(alog_ref[...])
    nt = (((1,), (1,)), ((), ()))

    def chunk(c, carry):
        r0 = pl.multiple_of(c * L, L)
        b_in = bc_ref[pl.ds(r0, L), 0:SSD_GROUPS * SSD_STATE]
        c_in = bc_ref[pl.ds(r0, L), SSD_GROUPS * SSD_STATE:BC_WIDTH]

        xdt = xh_ref[pl.ds(r0, L), 0:XH_SKIP]
        dt = dt_ref[pl.ds(r0, L), :]
        acum = _cumsum_rows(dt * a_neg)
        acum_t = acum.T
        acum_b = [jnp.broadcast_to(acum[:, h:h + 1], (L, LANES)) for h in range(SSD_HEADS)]
        acum_e = jnp.concatenate([jnp.where(low_half, acum_b[2 * pr], acum_b[2 * pr + 1])
                                  for pr in range(SSD_HEADS // 2)], axis=1)
        last_e = acum_e[L - 1:L, :]
        xdt_end = xdt * jnp.exp(last_e - acum_e)

        state = state_ref[...]
        new_state = state * jnp.exp(last_e)
        y_off = jnp.zeros((L, SSD_WIDTH), F32)
        cb = []
        for g in range(SSD_GROUPS):
            in_g = (lane_w >= g * gw) & (lane_w < (g + 1) * gw)
            bg = b_in[:, g * SSD_STATE:(g + 1) * SSD_STATE]
            cg = c_in[:, g * SSD_STATE:(g + 1) * SSD_STATE]
            cb.append(lax.dot_general(cg, bg, nt, preferred_element_type=F32))
            y_off = y_off + jnp.dot(cg, jnp.where(in_g[:SSD_STATE], state, 0.0).astype(BF16),
                                    preferred_element_type=F32)
            new_state = new_state + jnp.dot(bg.astype(F32).T.astype(BF16),
                                            jnp.where(in_g, xdt_end, 0.0).astype(BF16),
                                            preferred_element_type=F32)
        state_ref[...] = new_state

        y_diag = []
        for pr in range(SSD_HEADS // 2):
            xp = xdt[:, pr * LANES:(pr + 1) * LANES].astype(BF16)
            halves = []
            for h in (2 * pr, 2 * pr + 1):
                decay = jnp.exp(jnp.where(tril, acum_b[h] - acum_t[h:h + 1, :], NEG_INF))
                halves.append(jnp.dot((cb[h // per_group] * decay).astype(BF16), xp,
                                      preferred_element_type=F32))
            y_diag.append(jnp.where(low_half, halves[0], halves[1]))
        y = (jnp.concatenate(y_diag, axis=1) + xh_ref[pl.ds(r0, L), XH_SKIP:XH_WIDTH]
             + y_off * jnp.exp(acum_e))

        y = y * zs_ref[pl.ds(r0, L), :]
        y2 = y * y
        inv = jnp.zeros((L, SSD_WIDTH), F32)
        for g in range(SSD_GROUPS):
            in_g = (lane_w >= g * gw) & (lane_w < (g + 1) * gw)
            ms = jnp.sum(jnp.where(in_g, y2, 0.0), axis=-1, keepdims=True) * (1.0 / gw)
            inv = jnp.where(in_g, lax.rsqrt(ms + NORM_EPS), inv)
        o_ref[pl.ds(r0, L), :] = y * inv * nw_ref[...]
        return carry

    lax.fori_loop(0, n_chunks, chunk, 0, unroll=SSD_UNROLL)


def _ssd(xh3, bc3, zs3, dt3, alog, nw, layer):
    b, seq, _ = xh3.shape
    kern = functools.partial(_ssd_kernel, seq=seq)
    per_batch = lambda wd: pl.BlockSpec((None, seq, wd), lambda bi: (bi, 0, 0))
    return pl.pallas_call(
        kern,
        grid=(b,),
        in_specs=[per_batch(XH_WIDTH), per_batch(BC_WIDTH), per_batch(SSD_WIDTH),
                  per_batch(DT_PAD),
                  _resident((1, DT_PAD), layer), _resident((1, SSD_WIDTH), layer)],
        out_specs=per_batch(SSD_WIDTH),
        out_shape=jax.ShapeDtypeStruct((b, seq, SSD_WIDTH), F32),
        scratch_shapes=[pltpu.VMEM((SSD_STATE, SSD_WIDTH), F32)],
        compiler_params=pltpu.CompilerParams(dimension_semantics=("arbitrary",),
                                             vmem_limit_bytes=VMEM_LIMIT),
        name="ssd_mixer",
    )(xh3, bc3, zs3, dt3, alog, nw)


def _out_ffn_kernel(x_ref, ya_ref, ys_ref, yp_ref, wo_ref, g_ref, wgu_ref, wd_ref, gf_ref, o_ref,
                    *, final):
    y = jnp.concatenate([ya_ref[...], ys_ref[...], yp_ref[...]], axis=1).astype(BF16)
    x = x_ref[...] + jnp.dot(y, wo_ref[...], preferred_element_type=F32)
    h = _rmsnorm(x, g_ref[...]).astype(BF16)
    gate = jnp.dot(h, wgu_ref[:, :FFN_HIDDEN], preferred_element_type=F32)
    up = jnp.dot(h, wgu_ref[:, FFN_HIDDEN:], preferred_element_type=F32)
    a = (_silu(gate) * up).astype(BF16)
    acc = x + jnp.dot(a, wd_ref[...], preferred_element_type=F32)
    if final:
        acc = _rmsnorm(acc, gf_ref[...])
    o_ref[...] = acc


def _out_ffn(x2, ya, ys, yp, wo, g, wgu, wd, gf, final, layer):
    m = x2.shape[0]
    kern = functools.partial(_out_ffn_kernel, final=final)
    rows = lambda wd_: pl.BlockSpec((ROW_TILE, wd_), lambda i: (i, 0))
    return pl.pallas_call(
        kern,
        grid=(m // ROW_TILE,),
        in_specs=[rows(D_MODEL), rows(ATTN_WIDTH), rows(SSD_WIDTH), rows(POOL_WIDTH),
                  _resident((MIX_WIDTH, D_MODEL)), _resident((1, D_MODEL), layer),
                  _resident((D_MODEL, 2 * FFN_HIDDEN)), _resident((FFN_HIDDEN, D_MODEL)),
                  _resident((1, D_MODEL))],
        out_specs=rows(D_MODEL),
        out_shape=jax.ShapeDtypeStruct((m, D_MODEL), F32),
        compiler_params=pltpu.CompilerParams(dimension_semantics=("arbitrary",),
                                             vmem_limit_bytes=VMEM_LIMIT),
        name="out_ffn",
    )(x2, ya, ys, yp, wo, g, wgu, wd, gf)


def _pad_lanes(v, width):
    return jnp.pad(v, [(0, 0)] * (v.ndim - 1) + [(0, width - v.shape[-1])])


def kernel(x, norm_mix, w_in, conv_w, conv_b, dt_bias, a_log, d_skip, ssd_norm, pool_w, pool_scale,
           w_out, norm_ffn, w_gate_up, w_down, norm_final):
    bsz, seq, d = x.shape
    depth = w_in.shape[0]
    m = bsz * seq
    w_b = w_in.astype(BF16)
    unit = lambda lo, u: w_b[:, :, lo + u * LANES:lo + (u + 1) * LANES]
    order = sorted([("x", u) for u in range(XBC_UNITS)] + [("q", u) for u in range(QKV_UNITS)],
                   key=lambda ku: _mixed_unit(*ku))
    w_in_p = jnp.concatenate(
        [_pad_lanes(w_b[:, :, W_IN_DT:W_IN_P], DT_PAD)]
        + [unit(W_IN_XBC if kind == "x" else 0, u) for kind, u in order]
        + [w_b[:, :, W_IN_Z:W_IN_XBC], w_b[:, :, W_IN_P:]], axis=-1)
    eye = jnp.eye(len(POOL_WINDOWS), dtype=F32)
    pool_bd = (eye[None, :, None, :, None] * pool_w[:, :, :, None, :]).reshape(
        depth, POOL_WIDTH, POOL_WIDTH).astype(BF16)
    row = lambda v: v.reshape(depth, 1, v.shape[-1])
    dtb_p = row(_pad_lanes(dt_bias, DT_PAD))
    alog_p = row(_pad_lanes(a_log, DT_PAD))
    dskip_e = row(jnp.repeat(d_skip, HEAD_DIM, axis=-1))
    g_mix, g_ffn, conv_b3 = row(norm_mix), row(norm_ffn), row(conv_b)
    pool_s3, ssd_norm3 = row(pool_scale), row(ssd_norm)

    x2 = x.reshape(m, d)
    gf = norm_final.reshape(1, d)
    for l in range(depth):
        qk, v, zs, xh, bc, dt, yp, wo_b, wgu_b, wd_b = _in_proj(
            x2, g_mix, w_in_p, conv_w, conv_b3, dtb_p, dskip_e, pool_bd, pool_s3,
            w_out, w_gate_up, w_down, seq, l)
        ya = _attention(qk.reshape(bsz, seq, 2 * ATTN_WIDTH), v.reshape(bsz, seq, ATTN_WIDTH))
        ys = _ssd(xh.reshape(bsz, seq, XH_WIDTH), bc.reshape(bsz, seq, BC_WIDTH),
                  zs.reshape(bsz, seq, SSD_WIDTH), dt.reshape(bsz, seq, DT_PAD),
                  alog_p, ssd_norm3, l)
        x2 = _out_ffn(x2, ya.reshape(m, ATTN_WIDTH), ys.reshape(m, SSD_WIDTH), yp,
                      wo_b, g_ffn, wgu_b, wd_b, gf, final=(l == depth - 1), layer=l)
    return x2.reshape(bsz, seq, d)
```

```python
import functools
import math

import jax
import jax.numpy as jnp
from jax import lax
from jax.experimental import pallas as pl
from jax.experimental.pallas import tpu as pltpu

F32 = jnp.float32
BF16 = jnp.bfloat16

D_MODEL = 1024
ATTN_HEADS = 6
HEAD_DIM = 64
ATTN_WIDTH = ATTN_HEADS * HEAD_DIM
MOBA_BLOCK = 256
MOBA_TOPK = 3
SSD_HEADS = 6
SSD_WIDTH = SSD_HEADS * HEAD_DIM
SSD_GROUPS = 2
SSD_STATE = 128
SSD_CONV = 4
SSD_CHUNK = 128
XBC_WIDTH = SSD_WIDTH + 2 * SSD_GROUPS * SSD_STATE
POOL_WINDOWS = (2, 4, 8, 16)
POOL_WIDTH = 256
MIX_WIDTH = 1024
FFN_HIDDEN = 2816
NORM_EPS = 1e-6
NEG_INF = -1e30

LANES = 128
SUBLANES = 8
QKV_W = 3 * ATTN_WIDTH
DT_PAD = LANES
XBC_UNITS = XBC_WIDTH // LANES
X_UNITS = SSD_WIDTH // LANES
QKV_UNITS = QKV_W // LANES
COL_Z = DT_PAD + XBC_WIDTH + QKV_W
COL_P = COL_Z + SSD_WIDTH
IN_PROJ_PAD = COL_P + POOL_WIDTH


def _mixed_unit(kind, u=0):
    if kind == "dt":
        return 0
    if kind == "x":
        return 1 + 2 * ((u - X_UNITS) % XBC_UNITS)
    return 2 + 2 * u if u < XBC_UNITS else 1 + XBC_UNITS + u


ROW_TILE = 512
SSD_UNROLL = 4
QK_LOOKAHEAD = 6
VMEM_LIMIT = 56 * 1024 * 1024


def _resident(shape, layer=None):
    nd = len(shape)
    if layer is None:
        return pl.BlockSpec(shape, lambda *_: (0,) * nd, pipeline_mode=pl.Buffered(1))
    return pl.BlockSpec((None,) + tuple(shape), lambda *_: (layer,) + (0,) * nd,
                        pipeline_mode=pl.Buffered(1))


def _rmsnorm(x, g):
    ms = jnp.mean(x * x, axis=-1, keepdims=True)
    return x * lax.rsqrt(ms + NORM_EPS) * g


def _softplus(x):
    return jnp.maximum(x, 0.0) + jnp.log1p(jnp.exp(-jnp.abs(x)))


def _silu(x):
    return x * jax.nn.sigmoid(x)


def _rows_back(x, tail, k):
    xw = jnp.concatenate([tail, x], axis=0)
    return pltpu.roll(xw, k, 0)[tail.shape[0]:, :]


def _expand_heads(cols):
    rows = cols.shape[0]
    lane = lax.broadcasted_iota(jnp.int32, (rows, LANES), 1)
    out = []
    for pair in range(SSD_HEADS // 2):
        lo = jnp.broadcast_to(cols[:, 2 * pair:2 * pair + 1], (rows, LANES))
        hi = jnp.broadcast_to(cols[:, 2 * pair + 1:2 * pair + 2], (rows, LANES))
        out.append(jnp.where(lane < HEAD_DIM, lo, hi))
    return jnp.concatenate(out, axis=1)


XH_SKIP = SSD_WIDTH
XH_WIDTH = 2 * SSD_WIDTH
BC_WIDTH = 2 * SSD_GROUPS * SSD_STATE


W_IN_Z = QKV_W
W_IN_XBC = W_IN_Z + SSD_WIDTH
W_IN_DT = W_IN_XBC + XBC_WIDTH
W_IN_P = W_IN_DT + SSD_HEADS
W_IN_WIDTH = W_IN_P + POOL_WIDTH


def _in_proj_kernel(x_ref, g_ref, w_main_ref, w_tail_ref, cw_ref, cb_ref, dtb_ref, dskip_ref,
                    pw_ref, ps_ref, wo_f_ref, wgu_f_ref, wd_f_ref,
                    qk_ref, v_ref, zs_ref, xh_ref, bc_ref, dt_ref, yp_ref,
                    wo_b_ref, wgu_b_ref, wd_b_ref, ctail_ref, ptail_ref, w_ref, *, tiles_per_seq):
    @pl.when(pl.program_id(0) == 0)
    def _():
        w_ref[:, 0:DT_PAD] = w_tail_ref[:, 0:DT_PAD]
        for kind, lo, n in (("q", 0, QKV_UNITS), ("x", W_IN_XBC, XBC_UNITS)):
            for u in range(n):
                k = _mixed_unit(kind, u)
                w_ref[:, k * LANES:(k + 1) * LANES] = w_main_ref[:, lo + u * LANES:lo + (u + 1) * LANES]
        w_ref[:, COL_Z:COL_P] = w_main_ref[:, W_IN_Z:W_IN_XBC]
        w_ref[:, COL_P:IN_PROJ_PAD] = w_tail_ref[:, DT_PAD:]

    wo_b_ref[...] = wo_f_ref[...].astype(BF16)
    wgu_b_ref[...] = wgu_f_ref[...].astype(BF16)
    wd_b_ref[...] = wd_f_ref[...].astype(BF16)

    seq_tile = lax.rem(pl.program_id(0), tiles_per_seq)

    @pl.when(seq_tile == 0)
    def _():
        ctail_ref[...] = jnp.zeros(ctail_ref.shape, F32)
        ptail_ref[...] = jnp.zeros(ptail_ref.shape, F32)

    h = _rmsnorm(x_ref[...], g_ref[...]).astype(BF16)

    mix = jnp.dot(h, w_ref[:, 0:COL_Z], preferred_element_type=F32)
    p = jnp.dot(h, w_ref[:, COL_P:IN_PROJ_PAD], preferred_element_type=F32)
    z = jnp.dot(h, w_ref[:, COL_Z:COL_P], preferred_element_type=F32)

    def tile(kind, u=0):
        k = _mixed_unit(kind, u)
        return mix[:, k * LANES:(k + 1) * LANES]

    dt = _softplus(tile("dt") + dtb_ref[...])
    dt_ref[...] = dt
    dt_e = _expand_heads(dt)

    cw = cw_ref[...]
    n_ct = ctail_ref.shape[0]
    for u in sorted(range(XBC_UNITS), key=lambda v: _mixed_unit("x", v)):
        cols = slice(u * LANES, (u + 1) * LANES)
        xu = tile("x", u)
        conv = cb_ref[:, cols] + cw[SSD_CONV - 1:SSD_CONV, cols] * xu
        for j in range(SSD_CONV - 1):
            conv = conv + cw[j:j + 1, cols] * _rows_back(xu, ctail_ref[:, cols], SSD_CONV - 1 - j)
        ctail_ref[:, cols] = xu[ROW_TILE - n_ct:, :]
        act = _silu(conv)
        if u < X_UNITS:
            xh_ref[:, cols] = act * dt_e[:, cols]
            xh_ref[:, XH_SKIP + u * LANES:XH_SKIP + (u + 1) * LANES] = act * dskip_ref[:, cols]
        else:
            lo = (u - X_UNITS) * LANES
            bc_ref[:, lo:lo + LANES] = act.astype(BF16)
    qk_units = 2 * ATTN_WIDTH // LANES
    for u in range(QKV_UNITS):
        if u < qk_units:
            qk_ref[:, u * LANES:(u + 1) * LANES] = tile("q", u)
        else:
            v_ref[:, (u - qk_units) * LANES:(u - qk_units + 1) * LANES] = tile("q", u).astype(BF16)

    n_tail = ptail_ref.shape[0]
    win = jnp.concatenate([ptail_ref[...], p], axis=0)
    ptail_ref[...] = p[ROW_TILE - n_tail:, :]
    pos = seq_tile * ROW_TILE + lax.broadcasted_iota(jnp.int32, p.shape, 0)
    lane = lax.broadcasted_iota(jnp.int32, p.shape, 1)
    total = jnp.zeros(p.shape, F32)
    cnt = jnp.ones(p.shape, F32)
    width = 1
    for gi, w in enumerate(POOL_WINDOWS):
        while width < w:
            win = win + pltpu.roll(win, width, 0)
            width *= 2
        in_g = (lane >= gi * HEAD_DIM) & (lane < (gi + 1) * HEAD_DIM)
        total = jnp.where(in_g, win[n_tail:, :], total)
        cnt = jnp.where(in_g, jnp.minimum(pos + 1, w).astype(F32), cnt)
    d = total / cnt - p
    zs_ref[...] = _silu(z)
    yp_ref[...] = jnp.dot(d.astype(BF16), pw_ref[...], preferred_element_type=F32) * ps_ref[...]


def _slab_specs(rows, cols, steps, layer):
    bf16_rows = 2 * SUBLANES
    n_slabs = max(n for n in range(1, steps + 1)
                  if steps % n == 0 and rows % (n * bf16_rows) == 0)
    slab = rows // n_slabs
    per = steps // n_slabs
    src = pl.BlockSpec((None, slab, cols), lambda i: (layer, i // per, 0))
    dst = pl.BlockSpec((slab, cols), lambda i: (i // per, 0))
    return src, dst


def _in_proj(x2, g, w_main, w_tail, cw, cb, dtb, dskip, pw, ps, wo, wgu, wd, seq, layer):
    m = x2.shape[0]
    steps = m // ROW_TILE
    outs = ((2 * ATTN_WIDTH, F32), (ATTN_WIDTH, BF16), (SSD_WIDTH, F32), (XH_WIDTH, F32),
            (BC_WIDTH, BF16), (DT_PAD, F32), (POOL_WIDTH, F32))
    slabs = [_slab_specs(a.shape[1], a.shape[2], steps, layer) for a in (wo, wgu, wd)]
    kern = functools.partial(_in_proj_kernel, tiles_per_seq=seq // ROW_TILE)
    return pl.pallas_call(
        kern,
        grid=(steps,),
        in_specs=[pl.BlockSpec((ROW_TILE, D_MODEL), lambda i: (i, 0)),
                  _resident((1, D_MODEL), layer),
                  _resident((D_MODEL, W_IN_DT), layer),
                  _resident((D_MODEL, DT_PAD + POOL_WIDTH), layer),
                  _resident((SSD_CONV, XBC_WIDTH), layer), _resident((1, XBC_WIDTH), layer),
                  _resident((1, DT_PAD), layer), _resident((1, SSD_WIDTH), layer),
                  _resident((POOL_WIDTH, POOL_WIDTH), layer), _resident((1, POOL_WIDTH), layer)]
        + [src for src, _ in slabs],
        out_specs=[pl.BlockSpec((ROW_TILE, wd_), lambda i: (i, 0)) for wd_, _ in outs]
        + [dst for _, dst in slabs],
        out_shape=[jax.ShapeDtypeStruct((m, wd_), dt) for wd_, dt in outs]
        + [jax.ShapeDtypeStruct(a.shape[1:], BF16) for a in (wo, wgu, wd)],
        scratch_shapes=[pltpu.VMEM((SUBLANES, XBC_WIDTH), F32),
                        pltpu.VMEM((2 * SUBLANES, POOL_WIDTH), F32),
                        pltpu.VMEM((D_MODEL, IN_PROJ_PAD), BF16)],
        compiler_params=pltpu.CompilerParams(dimension_semantics=("arbitrary",),
                                             vmem_limit_bytes=VMEM_LIMIT),
        name="in_proj",
    )(x2, g, w_main, w_tail, cw, cb, dtb, dskip, pw, ps, wo, wgu, wd)


def _moba_bias(gate_t, blk):
    nb = gate_t.shape[0]
    blk_id = lax.broadcasted_iota(jnp.int32, gate_t.shape, 0)
    valid = blk_id < blk
    g = jnp.where(valid, gate_t, NEG_INF)
    rank = jnp.zeros(gate_t.shape, jnp.int32)
    for m in range(nb):
        gm = g[m:m + 1, :]
        beats = (gm > g) | ((gm == g) & (m < blk_id))
        rank = rank + beats.astype(jnp.int32)
    sel = valid & (rank < MOBA_TOPK)
    return jnp.where(sel, 0.0, NEG_INF).astype(F32)


def _attn_kernel(q_ref, k_ref, v_ref, o_ref, kb_ref, vta_ref, vtb_ref, kmean_ref, *, seq):
    n_blocks = seq // MOBA_BLOCK
    tq = MOBA_BLOCK
    k = k_ref[...]
    kmean_ref[...] = jnp.mean(k.reshape(n_blocks, MOBA_BLOCK, LANES), axis=1)
    kb_ref[...] = k.astype(BF16)
    vt = v_ref[...].astype(F32).T
    row = lax.broadcasted_iota(jnp.int32, vt.shape, 0)
    vta_ref[...] = jnp.where(row < HEAD_DIM, vt, jnp.where(row == HEAD_DIM, 1.0, 0.0)).astype(BF16)
    vtb_ref[...] = jnp.where(row >= HEAD_DIM, vt, jnp.where(row == 0, 1.0, 0.0)).astype(BF16)

    lane = lax.broadcasted_iota(jnp.int32, (tq, LANES), 1)
    half = tq // 2
    tri = (lax.broadcasted_iota(jnp.int32, (half, half), 0)
           <= lax.broadcasted_iota(jnp.int32, (half, half), 1))
    out_row = lax.broadcasted_iota(jnp.int32, (LANES, tq), 0)
    nt = (((1,), (1,)), ((), ()))
    scale2 = HEAD_DIM ** -0.5 * math.log2(math.e)

    class Tile:
        def __init__(self, i, head):
            self.i, self.head = i, head
            self.vt_ref = vta_ref if head == 0 else vtb_ref
            self.blocks, self.m, self.acc = [], None, None

        def score_step(self, n):
            i = self.i
            if n == 0:
                q = q_ref[i * tq:(i + 1) * tq, :]
                in_head = (lane < HEAD_DIM) if self.head == 0 else (lane >= HEAD_DIM)
                qh = jnp.where(in_head, q, 0.0)
                self.qs = (qh * scale2).astype(BF16)
                self.bias = None
                if i > MOBA_TOPK:
                    gate_t = lax.dot_general(kmean_ref[...], qh, nt, preferred_element_type=F32,
                                             precision=lax.Precision.HIGHEST)
                    self.bias = _moba_bias(gate_t, i)
            s = lax.dot_general(kb_ref[n * tq:(n + 1) * tq, :], self.qs, nt,
                                preferred_element_type=F32)
            if n == i:
                top_l = jnp.where(tri, s[:half, :half], NEG_INF)
                top_r = s[:half, half:]
                bot_r = jnp.where(tri, s[half:, half:], NEG_INF)
                bm = jnp.concatenate(
                    [jnp.max(top_l, axis=0, keepdims=True),
                     jnp.maximum(jnp.max(top_r, axis=0, keepdims=True),
                                 jnp.max(bot_r, axis=0, keepdims=True))], axis=1)
                self.blocks.append((top_l, top_r, bot_r))
            else:
                bm = jnp.max(s, axis=0, keepdims=True)
                if self.bias is not None:
                    bm = bm + self.bias[n:n + 1, :]
                self.blocks.append(s)
            self.m = bm if self.m is None else jnp.maximum(self.m, bm)

        def pv_step(self, n):
            m = self.m
            if n == self.i:
                top_l, top_r, bot_r = self.blocks[n]
                m_l, m_r = m[:, :half], m[:, half:]
                p_bot_r = jnp.exp2(bot_r - m_r)
                p = jnp.concatenate(
                    [jnp.concatenate([jnp.exp2(top_l - m_l), jnp.exp2(top_r - m_r)], axis=1),
                     jnp.concatenate([jnp.zeros_like(p_bot_r), p_bot_r], axis=1)], axis=0)
            else:
                if self.bias is not None:
                    m = m - self.bias[n:n + 1, :]
                p = jnp.exp2(self.blocks[n] - m)
            part = jnp.dot(self.vt_ref[:, n * tq:(n + 1) * tq], p.astype(BF16),
                           preferred_element_type=F32)
            self.acc = part if self.acc is None else self.acc + part

    tiles = [Tile(i, head) for i in range(n_blocks) for head in (0, 1)]
    for t in tiles[:QK_LOOKAHEAD]:
        for n in range(t.i + 1):
            t.score_step(n)
    done = []
    for j, t in enumerate(tiles):
        ahead = tiles[j + QK_LOOKAHEAD] if j + QK_LOOKAHEAD < len(tiles) else None
        n_ahead = ahead.i + 1 if ahead is not None else 0
        for n in range(max(t.i + 1, n_ahead)):
            if n < n_ahead:
                ahead.score_step(n)
            if n <= t.i:
                t.pv_step(n)
        done.append(t.acc)
        if t.head == 1:
            out_t = jnp.where(out_row < HEAD_DIM, done[0] / done[0][HEAD_DIM:HEAD_DIM + 1, :],
                              done[1] / done[1][0:1, :])
            o_ref[t.i * tq:(t.i + 1) * tq, :] = out_t.T
            done = []


def _attention(qk3, v3):
    b, seq, _ = qk3.shape
    n_pairs = ATTN_WIDTH // LANES
    kern = functools.partial(_attn_kernel, seq=seq)
    return pl.pallas_call(
        kern,
        grid=(b, n_pairs),
        in_specs=[pl.BlockSpec((None, seq, LANES), lambda bi, j: (bi, 0, j)),
                  pl.BlockSpec((None, seq, LANES), lambda bi, j: (bi, 0, n_pairs + j)),
                  pl.BlockSpec((None, seq, LANES), lambda bi, j: (bi, 0, j))],
        out_specs=pl.BlockSpec((None, seq, LANES), lambda bi, j: (bi, 0, j)),
        out_shape=jax.ShapeDtypeStruct((b, seq, ATTN_WIDTH), F32),
        scratch_shapes=[pltpu.VMEM((seq, LANES), BF16),
                        pltpu.VMEM((LANES, seq), BF16),
                        pltpu.VMEM((LANES, seq), BF16),
                        pltpu.VMEM((seq // MOBA_BLOCK, LANES), F32)],
        compiler_params=pltpu.CompilerParams(dimension_semantics=("arbitrary", "arbitrary"),
                                             vmem_limit_bytes=VMEM_LIMIT),
        name="moba_attn",
    )(qk3, qk3, v3)


def _cumsum_rows(x):
    n = x.shape[0]
    row = lax.broadcasted_iota(jnp.int32, x.shape, 0)
    k = 1
    while k < n:
        x = x + jnp.where(row >= k, pltpu.roll(x, k, 0), 0.0)
        k *= 2
    return x


def _ssd_kernel(xh_ref, bc_ref, zs_ref, dt_ref, alog_ref, nw_ref, o_ref, state_ref, *, seq):
    L = SSD_CHUNK
    n_chunks = seq // L
    gw = SSD_WIDTH // SSD_GROUPS
    per_group = SSD_HEADS // SSD_GROUPS
    state_ref[...] = jnp.zeros(state_ref.shape, F32)

    r_i = lax.broadcasted_iota(jnp.int32, (L, L), 0)
    c_i = lax.broadcasted_iota(jnp.int32, (L, L), 1)
    tril = r_i >= c_i
    lane_w = lax.broadcasted_iota(jnp.int32, (L, SSD_WIDTH), 1)
    low_half = lax.broadcasted_iota(jnp.int32, (L, LANES), 1) < HEAD_DIM
    a_neg = -jnp.exp(alog_ref[...])
    nt = (((1,), (1,)), ((), ()))

    def chunk(c, carry):
        r0 = pl.multiple_of(c * L, L)
        b_in = bc_ref[pl.ds(r0, L), 0:SSD_GROUPS * SSD_STATE]
        c_in = bc_ref[pl.ds(r0, L), SSD_GROUPS * SSD_STATE:BC_WIDTH]

        xdt = xh_ref[pl.ds(r0, L), 0:XH_SKIP]
        dt = dt_ref[pl.ds(r0, L), :]
        acum = _cumsum_rows(dt * a_neg)
        acum_t = acum.T
        acum_b = [jnp.broadcast_to(acum[:, h:h + 1], (L, LANES)) for h in range(SSD_HEADS)]
        acum_e = jnp.concatenate([jnp.where(low_half, acum_b[2 * pr], acum_b[2 * pr + 1])
                                  for pr in range(SSD_HEADS // 2)], axis=1)
        last_e = acum_e[L - 1:L, :]
        xdt_end = xdt * jnp.exp(last_e - acum_e)

        state = state_ref[...]
        new_state = state * jnp.exp(last_e)
        y_off = jnp.zeros((L, SSD_WIDTH), F32)
        cb = []
        for g in range(SSD_GROUPS):
            in_g = (lane_w >= g * gw) & (lane_w < (g + 1) * gw)
            bg = b_in[:, g * SSD_STATE:(g + 1) * SSD_STATE]
            cg = c_in[:, g * SSD_STATE:(g + 1) * SSD_STATE]
            cb.append(lax.dot_general(cg, bg, nt, preferred_element_type=F32))
            y_off = y_off + jnp.dot(cg, jnp.where(in_g[:SSD_STATE], state, 0.0).astype(BF16),
                                    preferred_element_type=F32)
            new_state = new_state + jnp.dot(bg.astype(F32).T.astype(BF16),
                                            jnp.where(in_g, xdt_end, 0.0).astype(BF16),
                                            preferred_element_type=F32)
        state_ref[...] = new_state

        y_diag = []
        for pr in range(SSD_HEADS // 2):
            xp = xdt[:, pr * LANES:(pr + 1) * LANES].astype(BF16)
            halves = []
            for h in (2 * pr, 2 * pr + 1):
                decay = jnp.exp(jnp.where(tril, acum_b[h] - acum_t[h:h + 1, :], NEG_INF))
                halves.append(jnp.dot((cb[h // per_group] * decay).astype(BF16), xp,
                                      preferred_element_type=F32))
            y_diag.append(jnp.where(low_half, halves[0], halves[1]))
        y = (jnp.concatenate(y_diag, axis=1) + xh_ref[pl.ds(r0, L), XH_SKIP:XH_WIDTH]
             + y_off * jnp.exp(acum_e))

        y = y * zs_ref[pl.ds(r0, L), :]
        y2 = y * y
        inv = jnp.zeros((L, SSD_WIDTH), F32)
        for g in range(SSD_GROUPS):
            in_g = (lane_w >= g * gw) & (lane_w < (g + 1) * gw)
            ms = jnp.sum(jnp.where(in_g, y2, 0.0), axis=-1, keepdims=True) * (1.0 / gw)
            inv = jnp.where(in_g, lax.rsqrt(ms + NORM_EPS), inv)
        o_ref[pl.ds(r0, L), :] = y * inv * nw_ref[...]
        return carry

    lax.fori_loop(0, n_chunks, chunk, 0, unroll=SSD_UNROLL)


def _ssd(xh3, bc3, zs3, dt3, alog, nw, layer):
    b, seq, _ = xh3.shape
    kern = functools.partial(_ssd_kernel, seq=seq)
    per_batch = lambda wd: pl.BlockSpec((None, seq, wd), lambda bi: (bi, 0, 0))
    return pl.pallas_call(
        kern,
        grid=(b,),
        in_specs=[per_batch(XH_WIDTH), per_batch(BC_WIDTH), per_batch(SSD_WIDTH),
                  per_batch(DT_PAD),
                  _resident((1, DT_PAD), layer), _resident((1, SSD_WIDTH), layer)],
        out_specs=per_batch(SSD_WIDTH),
        out_shape=jax.ShapeDtypeStruct((b, seq, SSD_WIDTH), F32),
        scratch_shapes=[pltpu.VMEM((SSD_STATE, SSD_WIDTH), F32)],
        compiler_params=pltpu.CompilerParams(dimension_semantics=("arbitrary",),
                                             vmem_limit_bytes=VMEM_LIMIT),
        name="ssd_mixer",
    )(xh3, bc3, zs3, dt3, alog, nw)


def _out_ffn_kernel(x_ref, ya_ref, ys_ref, yp_ref, wo_ref, g_ref, wgu_ref, wd_ref, gf_ref, o_ref,
                    *, final):
    y = jnp.concatenate([ya_ref[...], ys_ref[...], yp_ref[...]], axis=1).astype(BF16)
    x = x_ref[...] + jnp.dot(y, wo_ref[...], preferred_element_type=F32)
    h = _rmsnorm(x, g_ref[...]).astype(BF16)
    gate = jnp.dot(h, wgu_ref[:, :FFN_HIDDEN], preferred_element_type=F32)
    up = jnp.dot(h, wgu_ref[:, FFN_HIDDEN:], preferred_element_type=F32)
    a = (_silu(gate) * up).astype(BF16)
    acc = x + jnp.dot(a, wd_ref[...], preferred_element_type=F32)
    if final:
        acc = _rmsnorm(acc, gf_ref[...])
    o_ref[...] = acc


def _out_ffn(x2, ya, ys, yp, wo, g, wgu, wd, gf, final, layer):
    m = x2.shape[0]
    kern = functools.partial(_out_ffn_kernel, final=final)
    rows = lambda wd_: pl.BlockSpec((ROW_TILE, wd_), lambda i: (i, 0))
    return pl.pallas_call(
        kern,
        grid=(m // ROW_TILE,),
        in_specs=[rows(D_MODEL), rows(ATTN_WIDTH), rows(SSD_WIDTH), rows(POOL_WIDTH),
                  _resident((MIX_WIDTH, D_MODEL)), _resident((1, D_MODEL), layer),
                  _resident((D_MODEL, 2 * FFN_HIDDEN)), _resident((FFN_HIDDEN, D_MODEL)),
                  _resident((1, D_MODEL))],
        out_specs=rows(D_MODEL),
        out_shape=jax.ShapeDtypeStruct((m, D_MODEL), F32),
        compiler_params=pltpu.CompilerParams(dimension_semantics=("arbitrary",),
                                             vmem_limit_bytes=VMEM_LIMIT),
        name="out_ffn",
    )(x2, ya, ys, yp, wo, g, wgu, wd, gf)


def _pad_lanes(v, width):
    return jnp.pad(v, [(0, 0)] * (v.ndim - 1) + [(0, width - v.shape[-1])])


def kernel(x, norm_mix, w_in, conv_w, conv_b, dt_bias, a_log, d_skip, ssd_norm, pool_w, pool_scale,
           w_out, norm_ffn, w_gate_up, w_down, norm_final):
    bsz, seq, d = x.shape
    depth = w_in.shape[0]
    m = bsz * seq
    w_main = w_in[:, :, :W_IN_DT].astype(BF16)
    w_tail = jnp.concatenate([_pad_lanes(w_in[:, :, W_IN_DT:W_IN_P], DT_PAD), w_in[:, :, W_IN_P:]],
                             axis=-1).astype(BF16)
    eye = jnp.eye(len(POOL_WINDOWS), dtype=F32)
    pool_bd = (eye[None, :, None, :, None] * pool_w[:, :, :, None, :]).reshape(
        depth, POOL_WIDTH, POOL_WIDTH).astype(BF16)
    row = lambda v: v.reshape(depth, 1, v.shape[-1])
    dtb_p = row(_pad_lanes(dt_bias, DT_PAD))
    alog_p = row(_pad_lanes(a_log, DT_PAD))
    dskip_e = row(jnp.repeat(d_skip, HEAD_DIM, axis=-1))
    g_mix, g_ffn, conv_b3 = row(norm_mix), row(norm_ffn), row(conv_b)
    pool_s3, ssd_norm3 = row(pool_scale), row(ssd_norm)

    x2 = x.reshape(m, d)
    gf = norm_final.reshape(1, d)
    for l in range(depth):
        qk, v, zs, xh, bc, dt, yp, wo_b, wgu_b, wd_b = _in_proj(
            x2, g_mix, w_main, w_tail, conv_w, conv_b3, dtb_p, dskip_e, pool_bd, pool_s3,
            w_out, w_gate_up, w_down, seq, l)
        ya = _attention(qk.reshape(bsz, seq, 2 * ATTN_WIDTH), v.reshape(bsz, seq, ATTN_WIDTH))
        ys = _ssd(xh.reshape(bsz, seq, XH_WIDTH), bc.reshape(bsz, seq, BC_WIDTH),
                  zs.reshape(bsz, seq, SSD_WIDTH), dt.reshape(bsz, seq, DT_PAD),
                  alog_p, ssd_norm3, l)
        x2 = _out_ffn(x2, ya.reshape(m, ATTN_WIDTH), ys.reshape(m, SSD_WIDTH), yp,
                      wo_b, g_ffn, wgu_b, wd_b, gf, final=(l == depth - 1), layer=l)
    return x2.reshape(bsz, seq, d)
```

```python
import functools
import math

import jax
import jax.numpy as jnp
from jax import lax
from jax.experimental import pallas as pl
from jax.experimental.pallas import tpu as pltpu

F32 = jnp.float32
BF16 = jnp.bfloat16

D_MODEL = 1024
ATTN_HEADS = 6
HEAD_DIM = 64
ATTN_WIDTH = ATTN_HEADS * HEAD_DIM
MOBA_BLOCK = 256
MOBA_TOPK = 3
SSD_HEADS = 6
SSD_WIDTH = SSD_HEADS * HEAD_DIM
SSD_GROUPS = 2
SSD_STATE = 128
SSD_CONV = 4
SSD_CHUNK = 128
XBC_WIDTH = SSD_WIDTH + 2 * SSD_GROUPS * SSD_STATE
POOL_WINDOWS = (2, 4, 8, 16)
POOL_WIDTH = 256
MIX_WIDTH = 1024
FFN_HIDDEN = 2816
NORM_EPS = 1e-6
NEG_INF = -1e30

LANES = 128
SUBLANES = 8
QKV_W = 3 * ATTN_WIDTH
DT_PAD = LANES
XBC_UNITS = XBC_WIDTH // LANES
X_UNITS = SSD_WIDTH // LANES
QKV_UNITS = QKV_W // LANES
COL_Z = DT_PAD + XBC_WIDTH + QKV_W
COL_P = COL_Z + SSD_WIDTH
IN_PROJ_PAD = COL_P + POOL_WIDTH


def _mixed_unit(kind, u=0):
    if kind == "dt":
        return 0
    if kind == "x":
        return 1 + 2 * ((u - X_UNITS) % XBC_UNITS)
    return 2 + 2 * u if u < XBC_UNITS else 1 + XBC_UNITS + u


ROW_TILE = 512
SSD_UNROLL = 4
QK_LOOKAHEAD = 6
VMEM_LIMIT = 56 * 1024 * 1024


def _resident(shape, layer=None):
    nd = len(shape)
    if layer is None:
        return pl.BlockSpec(shape, lambda *_: (0,) * nd, pipeline_mode=pl.Buffered(1))
    return pl.BlockSpec((None,) + tuple(shape), lambda *_: (layer,) + (0,) * nd,
                        pipeline_mode=pl.Buffered(1))


def _rmsnorm(x, g):
    ms = jnp.mean(x * x, axis=-1, keepdims=True)
    return x * lax.rsqrt(ms + NORM_EPS) * g


def _softplus(x):
    return jnp.maximum(x, 0.0) + jnp.log1p(jnp.exp(-jnp.abs(x)))


def _silu(x):
    return x * jax.nn.sigmoid(x)


def _rows_back(x, tail, k):
    xw = jnp.concatenate([tail, x], axis=0)
    return pltpu.roll(xw, k, 0)[tail.shape[0]:, :]


def _expand_heads(cols):
    rows = cols.shape[0]
    lane = lax.broadcasted_iota(jnp.int32, (rows, LANES), 1)
    out = []
    for pair in range(SSD_HEADS // 2):
        lo = jnp.broadcast_to(cols[:, 2 * pair:2 * pair + 1], (rows, LANES))
        hi = jnp.broadcast_to(cols[:, 2 * pair + 1:2 * pair + 2], (rows, LANES))
        out.append(jnp.where(lane < HEAD_DIM, lo, hi))
    return jnp.concatenate(out, axis=1)


XH_SKIP = SSD_WIDTH
XH_WIDTH = 2 * SSD_WIDTH
BC_WIDTH = 2 * SSD_GROUPS * SSD_STATE


W_IN_Z = QKV_W
W_IN_XBC = W_IN_Z + SSD_WIDTH
W_IN_DT = W_IN_XBC + XBC_WIDTH
W_IN_P = W_IN_DT + SSD_HEADS
W_IN_WIDTH = W_IN_P + POOL_WIDTH


def _in_proj_kernel(x_ref, g_ref, w_main_ref, w_tail_ref, cw_ref, cb_ref, dtb_ref, dskip_ref,
                    pw_ref, ps_ref, wo_f_ref, wgu_f_ref, wd_f_ref,
                    qk_ref, v_ref, zs_ref, xh_ref, bc_ref, dt_ref, yp_ref,
                    wo_b_ref, wgu_b_ref, wd_b_ref, ctail_ref, ptail_ref, w_ref, *, tiles_per_seq):
    @pl.when(pl.program_id(0) == 0)
    def _():
        w_ref[:, 0:DT_PAD] = w_tail_ref[:, 0:DT_PAD]
        for kind, lo, n in (("q", 0, QKV_UNITS), ("x", W_IN_XBC, XBC_UNITS)):
            for u in range(n):
                k = _mixed_unit(kind, u)
                w_ref[:, k * LANES:(k + 1) * LANES] = w_main_ref[:, lo + u * LANES:lo + (u + 1) * LANES]
        w_ref[:, COL_Z:COL_P] = w_main_ref[:, W_IN_Z:W_IN_XBC]
        w_ref[:, COL_P:IN_PROJ_PAD] = w_tail_ref[:, DT_PAD:]

    wo_b_ref[...] = wo_f_ref[...].astype(BF16)
    wgu_b_ref[...] = wgu_f_ref[...].astype(BF16)
    wd_b_ref[...] = wd_f_ref[...].astype(BF16)

    seq_tile = lax.rem(pl.program_id(0), tiles_per_seq)

    @pl.when(seq_tile == 0)
    def _():
        ctail_ref[...] = jnp.zeros(ctail_ref.shape, F32)
        ptail_ref[...] = jnp.zeros(ptail_ref.shape, F32)

    h = _rmsnorm(x_ref[...], g_ref[...]).astype(BF16)

    mix = jnp.dot(h, w_ref[:, 0:COL_Z], preferred_element_type=F32)
    p = jnp.dot(h, w_ref[:, COL_P:IN_PROJ_PAD], preferred_element_type=F32)
    z = jnp.dot(h, w_ref[:, COL_Z:COL_P], preferred_element_type=F32)

    def tile(kind, u=0):
        k = _mixed_unit(kind, u)
        return mix[:, k * LANES:(k + 1) * LANES]

    dt = _softplus(tile("dt") + dtb_ref[...])
    dt_ref[...] = dt
    dt_e = _expand_heads(dt)

    cw = cw_ref[...]
    n_ct = ctail_ref.shape[0]
    for u in sorted(range(XBC_UNITS), key=lambda v: _mixed_unit("x", v)):
        cols = slice(u * LANES, (u + 1) * LANES)
        xu = tile("x", u)
        conv = cb_ref[:, cols] + cw[SSD_CONV - 1:SSD_CONV, cols] * xu
        for j in range(SSD_CONV - 1):
            conv = conv + cw[j:j + 1, cols] * _rows_back(xu, ctail_ref[:, cols], SSD_CONV - 1 - j)
        ctail_ref[:, cols] = xu[ROW_TILE - n_ct:, :]
        act = _silu(conv)
        if u < X_UNITS:
            xh_ref[:, cols] = act * dt_e[:, cols]
            xh_ref[:, XH_SKIP + u * LANES:XH_SKIP + (u + 1) * LANES] = act * dskip_ref[:, cols]
        else:
            lo = (u - X_UNITS) * LANES
            bc_ref[:, lo:lo + LANES] = act.astype(BF16)
    qk_units = 2 * ATTN_WIDTH // LANES
    for u in range(QKV_UNITS):
        if u < qk_units:
            qk_ref[:, u * LANES:(u + 1) * LANES] = tile("q", u)
        else:
            v_ref[:, (u - qk_units) * LANES:(u - qk_units + 1) * LANES] = tile("q", u).astype(BF16)

    n_tail = ptail_ref.shape[0]
    win = jnp.concatenate([ptail_ref[...], p], axis=0)
    ptail_ref[...] = p[ROW_TILE - n_tail:, :]
    pos = seq_tile * ROW_TILE + lax.broadcasted_iota(jnp.int32, p.shape, 0)
    lane = lax.broadcasted_iota(jnp.int32, p.shape, 1)
    total = jnp.zeros(p.shape, F32)
    cnt = jnp.ones(p.shape, F32)
    width = 1
    for gi, w in enumerate(POOL_WINDOWS):
        while width < w:
            win = win + pltpu.roll(win, width, 0)
            width *= 2
        in_g = (lane >= gi * HEAD_DIM) & (lane < (gi + 1) * HEAD_DIM)
        total = jnp.where(in_g, win[n_tail:, :], total)
        cnt = jnp.where(in_g, jnp.minimum(pos + 1, w).astype(F32), cnt)
    d = total / cnt - p
    zs_ref[...] = _silu(z)
    yp_ref[...] = jnp.dot(d.astype(BF16), pw_ref[...], preferred_element_type=F32) * ps_ref[...]


def _slab_specs(rows, cols, steps, layer):
    bf16_rows = 2 * SUBLANES
    n_slabs = max(n for n in range(1, steps + 1)
                  if steps % n == 0 and rows % (n * bf16_rows) == 0)
    slab = rows // n_slabs
    per = steps // n_slabs
    src = pl.BlockSpec((None, slab, cols), lambda i: (layer, i // per, 0))
    dst = pl.BlockSpec((slab, cols), lambda i: (i // per, 0))
    return src, dst


def _in_proj(x2, g, w_main, w_tail, cw, cb, dtb, dskip, pw, ps, wo, wgu, wd, seq, layer):
    m = x2.shape[0]
    steps = m // ROW_TILE
    outs = ((2 * ATTN_WIDTH, F32), (ATTN_WIDTH, BF16), (SSD_WIDTH, F32), (XH_WIDTH, F32),
            (BC_WIDTH, BF16), (DT_PAD, F32), (POOL_WIDTH, F32))
    slabs = [_slab_specs(a.shape[1], a.shape[2], steps, layer) for a in (wo, wgu, wd)]
    kern = functools.partial(_in_proj_kernel, tiles_per_seq=seq // ROW_TILE)
    return pl.pallas_call(
        kern,
        grid=(steps,),
        in_specs=[pl.BlockSpec((ROW_TILE, D_MODEL), lambda i: (i, 0)),
                  _resident((1, D_MODEL), layer),
                  _resident((D_MODEL, W_IN_DT), layer),
                  _resident((D_MODEL, DT_PAD + POOL_WIDTH), layer),
                  _resident((SSD_CONV, XBC_WIDTH), layer), _resident((1, XBC_WIDTH), layer),
                  _resident((1, DT_PAD), layer), _resident((1, SSD_WIDTH), layer),
                  _resident((POOL_WIDTH, POOL_WIDTH), layer), _resident((1, POOL_WIDTH), layer)]
        + [src for src, _ in slabs],
        out_specs=[pl.BlockSpec((ROW_TILE, wd_), lambda i: (i, 0)) for wd_, _ in outs]
        + [dst for _, dst in slabs],
        out_shape=[jax.ShapeDtypeStruct((m, wd_), dt) for wd_, dt in outs]
        + [jax.ShapeDtypeStruct(a.shape[1:], BF16) for a in (wo, wgu, wd)],
        scratch_shapes=[pltpu.VMEM((SUBLANES, XBC_WIDTH), F32),
                        pltpu.VMEM((2 * SUBLANES, POOL_WIDTH), F32),
                        pltpu.VMEM((D_MODEL, IN_PROJ_PAD), BF16)],
        compiler_params=pltpu.CompilerParams(dimension_semantics=("arbitrary",),
                                             vmem_limit_bytes=VMEM_LIMIT),
        name="in_proj",
    )(x2, g, w_main, w_tail, cw, cb, dtb, dskip, pw, ps, wo, wgu, wd)


def _moba_bias(gate_t, blk):
    nb = gate_t.shape[0]
    blk_id = lax.broadcasted_iota(jnp.int32, gate_t.shape, 0)
    valid = blk_id < blk
    g = jnp.where(valid, gate_t, NEG_INF)
    rank = jnp.zeros(gate_t.shape, jnp.int32)
    for m in range(nb):
        gm = g[m:m + 1, :]
        beats = (gm > g) | ((gm == g) & (m < blk_id))
        rank = rank + beats.astype(jnp.int32)
    sel = valid & (rank < MOBA_TOPK)
    return jnp.where(sel, 0.0, NEG_INF).astype(F32)


def _attn_kernel(qk_ref, v_ref, o_ref, kb_ref, vta_ref, vtb_ref, kmean_ref, *, seq):
    n_blocks = seq // MOBA_BLOCK
    n_pairs = ATTN_WIDTH // LANES
    tq = MOBA_BLOCK
    for pr in range(n_pairs):
        k = qk_ref[:, ATTN_WIDTH + pr * LANES:ATTN_WIDTH + (pr + 1) * LANES]
        kmean_ref[pr] = jnp.mean(k.reshape(n_blocks, MOBA_BLOCK, LANES), axis=1)
        kb_ref[pr] = k.astype(BF16)
        vt = v_ref[:, pr * LANES:(pr + 1) * LANES].astype(F32).T
        row = lax.broadcasted_iota(jnp.int32, vt.shape, 0)
        vta_ref[pr] = jnp.where(row < HEAD_DIM, vt,
                                jnp.where(row == HEAD_DIM, 1.0, 0.0)).astype(BF16)
        vtb_ref[pr] = jnp.where(row >= HEAD_DIM, vt, jnp.where(row == 0, 1.0, 0.0)).astype(BF16)

    lane = lax.broadcasted_iota(jnp.int32, (tq, LANES), 1)
    half = tq // 2
    tri = (lax.broadcasted_iota(jnp.int32, (half, half), 0)
           <= lax.broadcasted_iota(jnp.int32, (half, half), 1))
    out_row = lax.broadcasted_iota(jnp.int32, (LANES, tq), 0)
    nt = (((1,), (1,)), ((), ()))
    scale2 = HEAD_DIM ** -0.5 * math.log2(math.e)

    class Tile:
        def __init__(self, pr, i, head):
            self.pr, self.i, self.head = pr, i, head
            self.vt_ref = (vta_ref if head == 0 else vtb_ref).at[pr]
            self.kb_ref = kb_ref.at[pr]
            self.blocks, self.m, self.acc = [], None, None

        def score_step(self, n):
            i = self.i
            if n == 0:
                q = qk_ref[i * tq:(i + 1) * tq, self.pr * LANES:(self.pr + 1) * LANES]
                in_head = (lane < HEAD_DIM) if self.head == 0 else (lane >= HEAD_DIM)
                qh = jnp.where(in_head, q, 0.0)
                self.qs = (qh * scale2).astype(BF16)
                self.bias = None
                if i > MOBA_TOPK:
                    gate_t = lax.dot_general(kmean_ref[self.pr], qh, nt, preferred_element_type=F32,
                                             precision=lax.Precision.HIGHEST)
                    self.bias = _moba_bias(gate_t, i)
            s = lax.dot_general(self.kb_ref[n * tq:(n + 1) * tq, :], self.qs, nt,
                                preferred_element_type=F32)
            if n == i:
                top_l = jnp.where(tri, s[:half, :half], NEG_INF)
                top_r = s[:half, half:]
                bot_r = jnp.where(tri, s[half:, half:], NEG_INF)
                bm = jnp.concatenate(
                    [jnp.max(top_l, axis=0, keepdims=True),
                     jnp.maximum(jnp.max(top_r, axis=0, keepdims=True),
                                 jnp.max(bot_r, axis=0, keepdims=True))], axis=1)
                self.blocks.append((top_l, top_r, bot_r))
            else:
                bm = jnp.max(s, axis=0, keepdims=True)
                if self.bias is not None:
                    bm = bm + self.bias[n:n + 1, :]
                self.blocks.append(s)
            self.m = bm if self.m is None else jnp.maximum(self.m, bm)

        def pv_step(self, n):
            m = self.m
            if n == self.i:
                top_l, top_r, bot_r = self.blocks[n]
                m_l, m_r = m[:, :half], m[:, half:]
                p_bot_r = jnp.exp2(bot_r - m_r)
                p = jnp.concatenate(
                    [jnp.concatenate([jnp.exp2(top_l - m_l), jnp.exp2(top_r - m_r)], axis=1),
                     jnp.concatenate([jnp.zeros_like(p_bot_r), p_bot_r], axis=1)], axis=0)
            else:
                if self.bias is not None:
                    m = m - self.bias[n:n + 1, :]
                p = jnp.exp2(self.blocks[n] - m)
            part = jnp.dot(self.vt_ref[:, n * tq:(n + 1) * tq], p.astype(BF16),
                           preferred_element_type=F32)
            self.acc = part if self.acc is None else self.acc + part

    tiles = [Tile(pr, i, head) for pr in range(n_pairs) for i in range(n_blocks)
             for head in (0, 1)]
    for t in tiles[:QK_LOOKAHEAD]:
        for n in range(t.i + 1):
            t.score_step(n)
    done = []
    for j, t in enumerate(tiles):
        ahead = tiles[j + QK_LOOKAHEAD] if j + QK_LOOKAHEAD < len(tiles) else None
        n_ahead = ahead.i + 1 if ahead is not None else 0
        for n in range(max(t.i + 1, n_ahead)):
            if n < n_ahead:
                ahead.score_step(n)
            if n <= t.i:
                t.pv_step(n)
        done.append(t.acc)
        if t.head == 1:
            out_t = jnp.where(out_row < HEAD_DIM, done[0] / done[0][HEAD_DIM:HEAD_DIM + 1, :],
                              done[1] / done[1][0:1, :])
            o_ref[t.i * tq:(t.i + 1) * tq, t.pr * LANES:(t.pr + 1) * LANES] = out_t.T
            done = []


def _attention(qk3, v3):
    b, seq, _ = qk3.shape
    n_pairs = ATTN_WIDTH // LANES
    kern = functools.partial(_attn_kernel, seq=seq)
    per_batch = lambda wd: pl.BlockSpec((None, seq, wd), lambda bi: (bi, 0, 0))
    return pl.pallas_call(
        kern,
        grid=(b,),
        in_specs=[per_batch(2 * ATTN_WIDTH), per_batch(ATTN_WIDTH)],
        out_specs=per_batch(ATTN_WIDTH),
        out_shape=jax.ShapeDtypeStruct((b, seq, ATTN_WIDTH), F32),
        scratch_shapes=[pltpu.VMEM((n_pairs, seq, LANES), BF16),
                        pltpu.VMEM((n_pairs, LANES, seq), BF16),
                        pltpu.VMEM((n_pairs, LANES, seq), BF16),
                        pltpu.VMEM((n_pairs, seq // MOBA_BLOCK, LANES), F32)],
        compiler_params=pltpu.CompilerParams(dimension_semantics=("arbitrary",),
                                             vmem_limit_bytes=VMEM_LIMIT),
        name="moba_attn",
    )(qk3, v3)


def _cumsum_rows(x):
    n = x.shape[0]
    row = lax.broadcasted_iota(jnp.int32, x.shape, 0)
    k = 1
    while k < n:
        x = x + jnp.where(row >= k, pltpu.roll(x, k, 0), 0.0)
        k *= 2
    return x


def _ssd_kernel(xh_ref, bc_ref, zs_ref, dt_ref, alog_ref, nw_ref, o_ref, state_ref, *, seq):
    L = SSD_CHUNK
    n_chunks = seq // L
    gw = SSD_WIDTH // SSD_GROUPS
    per_group = SSD_HEADS // SSD_GROUPS
    state_ref[...] = jnp.zeros(state_ref.shape, F32)

    r_i = lax.broadcasted_iota(jnp.int32, (L, L), 0)
    c_i = lax.broadcasted_iota(jnp.int32, (L, L), 1)
    tril = r_i >= c_i
    lane_w = lax.broadcasted_iota(jnp.int32, (L, SSD_WIDTH), 1)
    low_half = lax.broadcasted_iota(jnp.int32, (L, LANES), 1) < HEAD_DIM
    a_neg = -jnp.exp(alog_ref[...])
    nt = (((1,), (1,)), ((), ()))

    def chunk(c, carry):
        r0 = pl.multiple_of(c * L, L)
        b_in = bc_ref[pl.ds(r0, L), 0:SSD_GROUPS * SSD_STATE]
        c_in = bc_ref[pl.ds(r0, L), SSD_GROUPS * SSD_STATE:BC_WIDTH]

        xdt = xh_ref[pl.ds(r0, L), 0:XH_SKIP]
        dt = dt_ref[pl.ds(r0, L), :]
        acum = _cumsum_rows(dt * a_neg)
        acum_t = acum.T
        acum_b = [jnp.broadcast_to(acum[:, h:h + 1], (L, LANES)) for h in range(SSD_HEADS)]
        acum_e = jnp.concatenate([jnp.where(low_half, acum_b[2 * pr], acum_b[2 * pr + 1])
                                  for pr in range(SSD_HEADS // 2)], axis=1)
        last_e = acum_e[L - 1:L, :]
        xdt_end = xdt * jnp.exp(last_e - acum_e)

        state = state_ref[...]
        new_state = state * jnp.exp(last_e)
        y_off = jnp.zeros((L, SSD_WIDTH), F32)
        cb = []
        for g in range(SSD_GROUPS):
            in_g = (lane_w >= g * gw) & (lane_w < (g + 1) * gw)
            bg = b_in[:, g * SSD_STATE:(g + 1) * SSD_STATE]
            cg = c_in[:, g * SSD_STATE:(g + 1) * SSD_STATE]
            cb.append(lax.dot_general(cg, bg, nt, preferred_element_type=F32))
            y_off = y_off + jnp.dot(cg, jnp.where(in_g[:SSD_STATE], state, 0.0).astype(BF16),
                                    preferred_element_type=F32)
            new_state = new_state + jnp.dot(bg.astype(F32).T.astype(BF16),
                                            jnp.where(in_g, xdt_end, 0.0).astype(BF16),
                                            preferred_element_type=F32)
        state_ref[...] = new_state

        y_diag = []
        for pr in range(SSD_HEADS // 2):
            xp = xdt[:, pr * LANES:(pr + 1) * LANES].astype(BF16)
            halves = []
            for h in (2 * pr, 2 * pr + 1):
                decay = jnp.exp(jnp.where(tril, acum_b[h] - acum_t[h:h + 1, :], NEG_INF))
                halves.append(jnp.dot((cb[h // per_group] * decay).astype(BF16), xp,
                                      preferred_element_type=F32))
            y_diag.append(jnp.where(low_half, halves[0], halves[1]))
        y = (jnp.concatenate(y_diag, axis=1) + xh_ref[pl.ds(r0, L), XH_SKIP:XH_WIDTH]
             + y_off * jnp.exp(acum_e))

        y = y * zs_ref[pl.ds(r0, L), :]
        y2 = y * y
        inv = jnp.zeros((L, SSD_WIDTH), F32)
        for g in range(SSD_GROUPS):
            in_g = (lane_w >= g * gw) & (lane_w < (g + 1) * gw)
            ms = jnp.sum(jnp.where(in_g, y2, 0.0), axis=-1, keepdims=True) * (1.0 / gw)
            inv = jnp.where(in_g, lax.rsqrt(ms + NORM_EPS), inv)
        o_ref[pl.ds(r0, L), :] = y * inv * nw_ref[...]
        return carry

    lax.fori_loop(0, n_chunks, chunk, 0, unroll=SSD_UNROLL)


def _ssd(xh3, bc3, zs3, dt3, alog, nw, layer):
    b, seq, _ = xh3.shape
    kern = functools.partial(_ssd_kernel, seq=seq)
    per_batch = lambda wd: pl.BlockSpec((None, seq, wd), lambda bi: (bi, 0, 0))
    return pl.pallas_call(
        kern,
        grid=(b,),
        in_specs=[per_batch(XH_WIDTH), per_batch(BC_WIDTH), per_batch(SSD_WIDTH),
                  per_batch(DT_PAD),
                  _resident((1, DT_PAD), layer), _resident((1, SSD_WIDTH), layer)],
        out_specs=per_batch(SSD_WIDTH),
        out_shape=jax.ShapeDtypeStruct((b, seq, SSD_WIDTH), F32),
        scratch_shapes=[pltpu.VMEM((SSD_STATE, SSD_WIDTH), F32)],
        compiler_params=pltpu.CompilerParams(dimension_semantics=("arbitrary",),
                                             vmem_limit_bytes=VMEM_LIMIT),
        name="ssd_mixer",
    )(xh3, bc3, zs3, dt3, alog, nw)


def _out_ffn_kernel(x_ref, ya_ref, ys_ref, yp_ref, wo_ref, g_ref, wgu_ref, wd_ref, gf_ref, o_ref,
                    *, final):
    y = jnp.concatenate([ya_ref[...], ys_ref[...], yp_ref[...]], axis=1).astype(BF16)
    x = x_ref[...] + jnp.dot(y, wo_ref[...], preferred_element_type=F32)
    h = _rmsnorm(x, g_ref[...]).astype(BF16)
    gate = jnp.dot(h, wgu_ref[:, :FFN_HIDDEN], preferred_element_type=F32)
    up = jnp.dot(h, wgu_ref[:, FFN_HIDDEN:], preferred_element_type=F32)
    a = (_silu(gate) * up).astype(BF16)
    acc = x + jnp.dot(a, wd_ref[...], preferred_element_type=F32)
    if final:
        acc = _rmsnorm(acc, gf_ref[...])
    o_ref[...] = acc


def _out_ffn(x2, ya, ys, yp, wo, g, wgu, wd, gf, final, layer):
    m = x2.shape[0]
    kern = functools.partial(_out_ffn_kernel, final=final)
    rows = lambda wd_: pl.BlockSpec((ROW_TILE, wd_), lambda i: (i, 0))
    return pl.pallas_call(
        kern,
        grid=(m // ROW_TILE,),
        in_specs=[rows(D_MODEL), rows(ATTN_WIDTH), rows(SSD_WIDTH), rows(POOL_WIDTH),
                  _resident((MIX_WIDTH, D_MODEL)), _resident((1, D_MODEL), layer),
                  _resident((D_MODEL, 2 * FFN_HIDDEN)), _resident((FFN_HIDDEN, D_MODEL)),
                  _resident((1, D_MODEL))],
        out_specs=rows(D_MODEL),
        out_shape=jax.ShapeDtypeStruct((m, D_MODEL), F32),
        compiler_params=pltpu.CompilerParams(dimension_semantics=("arbitrary",),
                                             vmem_limit_bytes=VMEM_LIMIT),
        name="out_ffn",
    )(x2, ya, ys, yp, wo, g, wgu, wd, gf)


def _pad_lanes(v, width):
    return jnp.pad(v, [(0, 0)] * (v.ndim - 1) + [(0, width - v.shape[-1])])


def kernel(x, norm_mix, w_in, conv_w, conv_b, dt_bias, a_log, d_skip, ssd_norm, pool_w, pool_scale,
           w_out, norm_ffn, w_gate_up, w_down, norm_final):
    bsz, seq, d = x.shape
    depth = w_in.shape[0]
    m = bsz * seq
    w_main = w_in[:, :, :W_IN_DT].astype(BF16)
    w_tail = jnp.concatenate([_pad_lanes(w_in[:, :, W_IN_DT:W_IN_P], DT_PAD), w_in[:, :, W_IN_P:]],
                             axis=-1).astype(BF16)
    eye = jnp.eye(len(POOL_WINDOWS), dtype=F32)
    pool_bd = (eye[None, :, None, :, None] * pool_w[:, :, :, None, :]).reshape(
        depth, POOL_WIDTH, POOL_WIDTH).astype(BF16)
    row = lambda v: v.reshape(depth, 1, v.shape[-1])
    dtb_p = row(_pad_lanes(dt_bias, DT_PAD))
    alog_p = row(_pad_lanes(a_log, DT_PAD))
    dskip_e = row(jnp.repeat(d_skip, HEAD_DIM, axis=-1))
    g_mix, g_ffn, conv_b3 = row(norm_mix), row(norm_ffn), row(conv_b)
    pool_s3, ssd_norm3 = row(pool_scale), row(ssd_norm)

    x2 = x.reshape(m, d)
    gf = norm_final.reshape(1, d)
    for l in range(depth):
        qk, v, zs, xh, bc, dt, yp, wo_b, wgu_b, wd_b = _in_proj(
            x2, g_mix, w_main, w_tail, conv_w, conv_b3, dtb_p, dskip_e, pool_bd, pool_s3,
            w_out, w_gate_up, w_down, seq, l)
        ya = _attention(qk.reshape(bsz, seq, 2 * ATTN_WIDTH), v.reshape(bsz, seq, ATTN_WIDTH))
        ys = _ssd(xh.reshape(bsz, seq, XH_WIDTH), bc.reshape(bsz, seq, BC_WIDTH),
                  zs.reshape(bsz, seq, SSD_WIDTH), dt.reshape(bsz, seq, DT_PAD),
                  alog_p, ssd_norm3, l)
        x2 = _out_ffn(x2, ya.reshape(m, ATTN_WIDTH), ys.reshape(m, SSD_WIDTH), yp,
                      wo_b, g_ffn, wgu_b, wd_b, gf, final=(l == depth - 1), layer=l)
    return x2.reshape(bsz, seq, d)
```

```python
import functools
import math

import jax
import jax.numpy as jnp
from jax import lax
from jax.experimental import pallas as pl
from jax.experimental.pallas import tpu as pltpu

F32 = jnp.float32
BF16 = jnp.bfloat16

D_MODEL = 1024
ATTN_HEADS = 6
HEAD_DIM = 64
ATTN_WIDTH = ATTN_HEADS * HEAD_DIM
MOBA_BLOCK = 256
MOBA_TOPK = 3
SSD_HEADS = 6
SSD_WIDTH = SSD_HEADS * HEAD_DIM
SSD_GROUPS = 2
SSD_STATE = 128
SSD_CONV = 4
SSD_CHUNK = 128
XBC_WIDTH = SSD_WIDTH + 2 * SSD_GROUPS * SSD_STATE
POOL_WINDOWS = (2, 4, 8, 16)
POOL_WIDTH = 256
MIX_WIDTH = 1024
FFN_HIDDEN = 2816
NORM_EPS = 1e-6
NEG_INF = -1e30

LANES = 128
SUBLANES = 8
QKV_W = 3 * ATTN_WIDTH
DT_PAD = LANES
XBC_UNITS = XBC_WIDTH // LANES
X_UNITS = SSD_WIDTH // LANES
QKV_UNITS = QKV_W // LANES
COL_Z = DT_PAD + XBC_WIDTH + QKV_W
COL_P = COL_Z + SSD_WIDTH
IN_PROJ_PAD = COL_P + POOL_WIDTH


def _mixed_unit(kind, u=0):
    if kind == "dt":
        return 0
    if kind == "x":
        return 1 + 2 * ((u - X_UNITS) % XBC_UNITS)
    return 2 + 2 * u if u < XBC_UNITS else 1 + XBC_UNITS + u


ROW_TILE = 512
SSD_UNROLL = 4
QK_LOOKAHEAD = 6
VMEM_LIMIT = 56 * 1024 * 1024


def _resident(shape, layer=None):
    nd = len(shape)
    if layer is None:
        return pl.BlockSpec(shape, lambda *_: (0,) * nd, pipeline_mode=pl.Buffered(1))
    return pl.BlockSpec((None,) + tuple(shape), lambda *_: (layer,) + (0,) * nd,
                        pipeline_mode=pl.Buffered(1))


def _rmsnorm(x, g):
    ms = jnp.mean(x * x, axis=-1, keepdims=True)
    return x * lax.rsqrt(ms + NORM_EPS) * g


def _softplus(x):
    return jnp.maximum(x, 0.0) + jnp.log1p(jnp.exp(-jnp.abs(x)))


def _silu(x):
    return x * jax.nn.sigmoid(x)


def _rows_back(x, tail, k):
    xw = jnp.concatenate([tail, x], axis=0)
    return pltpu.roll(xw, k, 0)[tail.shape[0]:, :]


def _expand_heads(cols):
    rows = cols.shape[0]
    lane = lax.broadcasted_iota(jnp.int32, (rows, LANES), 1)
    out = []
    for pair in range(SSD_HEADS // 2):
        lo = jnp.broadcast_to(cols[:, 2 * pair:2 * pair + 1], (rows, LANES))
        hi = jnp.broadcast_to(cols[:, 2 * pair + 1:2 * pair + 2], (rows, LANES))
        out.append(jnp.where(lane < HEAD_DIM, lo, hi))
    return jnp.concatenate(out, axis=1)


XH_SKIP = SSD_WIDTH
XH_WIDTH = 2 * SSD_WIDTH
BC_WIDTH = 2 * SSD_GROUPS * SSD_STATE


W_IN_Z = QKV_W
W_IN_XBC = W_IN_Z + SSD_WIDTH
W_IN_DT = W_IN_XBC + XBC_WIDTH
W_IN_P = W_IN_DT + SSD_HEADS
W_IN_WIDTH = W_IN_P + POOL_WIDTH


def _in_proj_kernel(x_ref, g_ref, w_main_ref, w_tail_ref, cw_ref, cb_ref, dtb_ref, dskip_ref,
                    pw_ref, ps_ref, wo_f_ref, wgu_f_ref, wd_f_ref,
                    qk_ref, v_ref, zs_ref, xh_ref, bc_ref, dt_ref, yp_ref,
                    wo_b_ref, wgu_b_ref, wd_b_ref, ctail_ref, ptail_ref, w_ref, *, tiles_per_seq):
    @pl.when(pl.program_id(0) == 0)
    def _():
        w_ref[:, 0:DT_PAD] = w_tail_ref[:, 0:DT_PAD]
        for kind, lo, n in (("q", 0, QKV_UNITS), ("x", W_IN_XBC, XBC_UNITS)):
            for u in range(n):
                k = _mixed_unit(kind, u)
                w_ref[:, k * LANES:(k + 1) * LANES] = w_main_ref[:, lo + u * LANES:lo + (u + 1) * LANES]
        w_ref[:, COL_Z:COL_P] = w_main_ref[:, W_IN_Z:W_IN_XBC]
        w_ref[:, COL_P:IN_PROJ_PAD] = w_tail_ref[:, DT_PAD:]

    wo_b_ref[...] = wo_f_ref[...].astype(BF16)
    wgu_b_ref[...] = wgu_f_ref[...].astype(BF16)
    wd_b_ref[...] = wd_f_ref[...].astype(BF16)

    seq_tile = lax.rem(pl.program_id(0), tiles_per_seq)

    @pl.when(seq_tile == 0)
    def _():
        ctail_ref[...] = jnp.zeros(ctail_ref.shape, F32)
        ptail_ref[...] = jnp.zeros(ptail_ref.shape, F32)

    h = _rmsnorm(x_ref[...], g_ref[...]).astype(BF16)

    mix = jnp.dot(h, w_ref[:, 0:COL_Z], preferred_element_type=F32)
    p = jnp.dot(h, w_ref[:, COL_P:IN_PROJ_PAD], preferred_element_type=F32)
    z = jnp.dot(h, w_ref[:, COL_Z:COL_P], preferred_element_type=F32)

    def tile(kind, u=0):
        k = _mixed_unit(kind, u)
        return mix[:, k * LANES:(k + 1) * LANES]

    dt = _softplus(tile("dt") + dtb_ref[...])
    dt_ref[...] = dt
    dt_e = _expand_heads(dt)

    cw = cw_ref[...]
    n_ct = ctail_ref.shape[0]
    for u in sorted(range(XBC_UNITS), key=lambda v: _mixed_unit("x", v)):
        cols = slice(u * LANES, (u + 1) * LANES)
        xu = tile("x", u)
        conv = cb_ref[:, cols] + cw[SSD_CONV - 1:SSD_CONV, cols] * xu
        for j in range(SSD_CONV - 1):
            conv = conv + cw[j:j + 1, cols] * _rows_back(xu, ctail_ref[:, cols], SSD_CONV - 1 - j)
        ctail_ref[:, cols] = xu[ROW_TILE - n_ct:, :]
        act = _silu(conv)
        if u < X_UNITS:
            xh_ref[:, cols] = act * dt_e[:, cols]
            xh_ref[:, XH_SKIP + u * LANES:XH_SKIP + (u + 1) * LANES] = act * dskip_ref[:, cols]
        else:
            lo = (u - X_UNITS) * LANES
            bc_ref[:, lo:lo + LANES] = act.astype(BF16)
    qk_units = 2 * ATTN_WIDTH // LANES
    for u in range(QKV_UNITS):
        if u < qk_units:
            qk_ref[:, u * LANES:(u + 1) * LANES] = tile("q", u)
        else:
            v_ref[:, (u - qk_units) * LANES:(u - qk_units + 1) * LANES] = tile("q", u).astype(BF16)

    n_tail = ptail_ref.shape[0]
    win = jnp.concatenate([ptail_ref[...], p], axis=0)
    ptail_ref[...] = p[ROW_TILE - n_tail:, :]
    pos = seq_tile * ROW_TILE + lax.broadcasted_iota(jnp.int32, p.shape, 0)
    lane = lax.broadcasted_iota(jnp.int32, p.shape, 1)
    total = jnp.zeros(p.shape, F32)
    cnt = jnp.ones(p.shape, F32)
    width = 1
    for gi, w in enumerate(POOL_WINDOWS):
        while width < w:
            win = win + pltpu.roll(win, width, 0)
            width *= 2
        in_g = (lane >= gi * HEAD_DIM) & (lane < (gi + 1) * HEAD_DIM)
        total = jnp.where(in_g, win[n_tail:, :], total)
        cnt = jnp.where(in_g, jnp.minimum(pos + 1, w).astype(F32), cnt)
    d = total / cnt - p
    zs_ref[...] = _silu(z)
    yp_ref[...] = jnp.dot(d.astype(BF16), pw_ref[...], preferred_element_type=F32) * ps_ref[...]


def _slab_specs(rows, cols, steps, layer):
    bf16_rows = 2 * SUBLANES
    n_slabs = max(n for n in range(1, steps + 1)
                  if steps % n == 0 and rows % (n * bf16_rows) == 0)
    slab = rows // n_slabs
    per = steps // n_slabs
    src = pl.BlockSpec((None, slab, cols), lambda i: (layer, i // per, 0))
    dst = pl.BlockSpec((slab, cols), lambda i: (i // per, 0))
    return src, dst


def _in_proj(x2, g, w_main, w_tail, cw, cb, dtb, dskip, pw, ps, wo, wgu, wd, seq, layer):
    m = x2.shape[0]
    steps = m // ROW_TILE
    outs = ((2 * ATTN_WIDTH, F32), (ATTN_WIDTH, BF16), (SSD_WIDTH, F32), (XH_WIDTH, F32),
            (BC_WIDTH, BF16), (DT_PAD, F32), (POOL_WIDTH, F32))
    slabs = [_slab_specs(a.shape[1], a.shape[2], steps, layer) for a in (wo, wgu, wd)]
    kern = functools.partial(_in_proj_kernel, tiles_per_seq=seq // ROW_TILE)
    return pl.pallas_call(
        kern,
        grid=(steps,),
        in_specs=[pl.BlockSpec((ROW_TILE, D_MODEL), lambda i: (i, 0)),
                  _resident((1, D_MODEL), layer),
                  _resident((D_MODEL, W_IN_DT), layer),
                  _resident((D_MODEL, DT_PAD + POOL_WIDTH), layer),
                  _resident((SSD_CONV, XBC_WIDTH), layer), _resident((1, XBC_WIDTH), layer),
                  _resident((1, DT_PAD), layer), _resident((1, SSD_WIDTH), layer),
                  _resident((POOL_WIDTH, POOL_WIDTH), layer), _resident((1, POOL_WIDTH), layer)]
        + [src for src, _ in slabs],
        out_specs=[pl.BlockSpec((ROW_TILE, wd_), lambda i: (i, 0)) for wd_, _ in outs]
        + [dst for _, dst in slabs],
        out_shape=[jax.ShapeDtypeStruct((m, wd_), dt) for wd_, dt in outs]
        + [jax.ShapeDtypeStruct(a.shape[1:], BF16) for a in (wo, wgu, wd)],
        scratch_shapes=[pltpu.VMEM((SUBLANES, XBC_WIDTH), F32),
                        pltpu.VMEM((2 * SUBLANES, POOL_WIDTH), F32),
                        pltpu.VMEM((D_MODEL, IN_PROJ_PAD), BF16)],
        compiler_params=pltpu.CompilerParams(dimension_semantics=("arbitrary",),
                                             vmem_limit_bytes=VMEM_LIMIT),
        name="in_proj",
    )(x2, g, w_main, w_tail, cw, cb, dtb, dskip, pw, ps, wo, wgu, wd)


def _moba_bias(gate_t, blk):
    nb = gate_t.shape[0]
    blk_id = lax.broadcasted_iota(jnp.int32, gate_t.shape, 0)
    valid = blk_id < blk
    g = jnp.where(valid, gate_t, NEG_INF)
    rank = jnp.zeros(gate_t.shape, jnp.int32)
    for m in range(nb):
        gm = g[m:m + 1, :]
        beats = (gm > g) | ((gm == g) & (m < blk_id))
        rank = rank + beats.astype(jnp.int32)
    sel = valid & (rank < MOBA_TOPK)
    return jnp.where(sel, 0.0, NEG_INF).astype(F32)


def _attn_kernel(q_ref, k_ref, v_ref, o_ref, kb_ref, vta_ref, vtb_ref, kmean_ref, *, seq):
    n_blocks = seq // MOBA_BLOCK
    tq = MOBA_BLOCK
    k = k_ref[...]
    kmean_ref[...] = jnp.mean(k.reshape(n_blocks, MOBA_BLOCK, LANES), axis=1)
    kb_ref[...] = k.astype(BF16)
    vt = v_ref[...].astype(F32).T
    row = lax.broadcasted_iota(jnp.int32, vt.shape, 0)
    vta_ref[...] = jnp.where(row < HEAD_DIM, vt, jnp.where(row == HEAD_DIM, 1.0, 0.0)).astype(BF16)
    vtb_ref[...] = jnp.where(row >= HEAD_DIM, vt, jnp.where(row == 0, 1.0, 0.0)).astype(BF16)

    lane = lax.broadcasted_iota(jnp.int32, (tq, LANES), 1)
    half = tq // 2
    tri = (lax.broadcasted_iota(jnp.int32, (half, half), 0)
           <= lax.broadcasted_iota(jnp.int32, (half, half), 1))
    out_row = lax.broadcasted_iota(jnp.int32, (LANES, tq), 0)
    nt = (((1,), (1,)), ((), ()))
    scale2 = HEAD_DIM ** -0.5 * math.log2(math.e)

    def exp2_bf16(d):
        return jnp.exp2(d.astype(BF16))

    class Tile:
        def __init__(self, i, head):
            self.i, self.head = i, head
            self.vt_ref = vta_ref if head == 0 else vtb_ref
            self.blocks, self.m, self.acc = [], None, None

        def score_step(self, n):
            i = self.i
            if n == 0:
                q = q_ref[i * tq:(i + 1) * tq, :]
                in_head = (lane < HEAD_DIM) if self.head == 0 else (lane >= HEAD_DIM)
                qh = jnp.where(in_head, q, 0.0)
                self.qs = (qh * scale2).astype(BF16)
                self.bias = None
                if i > MOBA_TOPK:
                    gate_t = lax.dot_general(kmean_ref[...], qh, nt, preferred_element_type=F32,
                                             precision=lax.Precision.HIGHEST)
                    self.bias = _moba_bias(gate_t, i)
            s = lax.dot_general(kb_ref[n * tq:(n + 1) * tq, :], self.qs, nt,
                                preferred_element_type=F32)
            if n == i:
                top_l = jnp.where(tri, s[:half, :half], NEG_INF)
                top_r = s[:half, half:]
                bot_r = jnp.where(tri, s[half:, half:], NEG_INF)
                bm = jnp.concatenate(
                    [jnp.max(top_l, axis=0, keepdims=True),
                     jnp.maximum(jnp.max(top_r, axis=0, keepdims=True),
                                 jnp.max(bot_r, axis=0, keepdims=True))], axis=1)
                self.blocks.append((top_l, top_r, bot_r))
            else:
                bm = jnp.max(s, axis=0, keepdims=True)
                if self.bias is not None:
                    bm = bm + self.bias[n:n + 1, :]
                self.blocks.append(s)
            self.m = bm if self.m is None else jnp.maximum(self.m, bm)

        def pv_step(self, n):
            m = self.m
            if n == self.i:
                top_l, top_r, bot_r = self.blocks[n]
                m_l, m_r = m[:, :half], m[:, half:]
                p_bot_r = exp2_bf16(bot_r - m_r)
                p = jnp.concatenate(
                    [jnp.concatenate([exp2_bf16(top_l - m_l), exp2_bf16(top_r - m_r)], axis=1),
                     jnp.concatenate([jnp.zeros_like(p_bot_r), p_bot_r], axis=1)], axis=0)
            else:
                if self.bias is not None:
                    m = m - self.bias[n:n + 1, :]
                p = exp2_bf16(self.blocks[n] - m)
            part = jnp.dot(self.vt_ref[:, n * tq:(n + 1) * tq], p, preferred_element_type=F32)
            self.acc = part if self.acc is None else self.acc + part

    tiles = [Tile(i, head) for i in range(n_blocks) for head in (0, 1)]
    for t in tiles[:QK_LOOKAHEAD]:
        for n in range(t.i + 1):
            t.score_step(n)
    done = []
    for j, t in enumerate(tiles):
        ahead = tiles[j + QK_LOOKAHEAD] if j + QK_LOOKAHEAD < len(tiles) else None
        n_ahead = ahead.i + 1 if ahead is not None else 0
        for n in range(max(t.i + 1, n_ahead)):
            if n < n_ahead:
                ahead.score_step(n)
            if n <= t.i:
                t.pv_step(n)
        done.append(t.acc)
        if t.head == 1:
            out_t = jnp.where(out_row < HEAD_DIM, done[0] / done[0][HEAD_DIM:HEAD_DIM + 1, :],
                              done[1] / done[1][0:1, :])
            o_ref[t.i * tq:(t.i + 1) * tq, :] = out_t.T
            done = []


def _attention(qk3, v3):
    b, seq, _ = qk3.shape
    n_pairs = ATTN_WIDTH // LANES
    kern = functools.partial(_attn_kernel, seq=seq)
    return pl.pallas_call(
        kern,
        grid=(b, n_pairs),
        in_specs=[pl.BlockSpec((None, seq, LANES), lambda bi, j: (bi, 0, j)),
                  pl.BlockSpec((None, seq, LANES), lambda bi, j: (bi, 0, n_pairs + j)),
                  pl.BlockSpec((None, seq, LANES), lambda bi, j: (bi, 0, j))],
        out_specs=pl.BlockSpec((None, seq, LANES), lambda bi, j: (bi, 0, j)),
        out_shape=jax.ShapeDtypeStruct((b, seq, ATTN_WIDTH), F32),
        scratch_shapes=[pltpu.VMEM((seq, LANES), BF16),
                        pltpu.VMEM((LANES, seq), BF16),
                        pltpu.VMEM((LANES, seq), BF16),
                        pltpu.VMEM((seq // MOBA_BLOCK, LANES), F32)],
        compiler_params=pltpu.CompilerParams(dimension_semantics=("arbitrary", "arbitrary"),
                                             vmem_limit_bytes=VMEM_LIMIT),
        name="moba_attn",
    )(qk3, qk3, v3)


def _cumsum_rows(x):
    n = x.shape[0]
    row = lax.broadcasted_iota(jnp.int32, x.shape, 0)
    k = 1
    while k < n:
        x = x + jnp.where(row >= k, pltpu.roll(x, k, 0), 0.0)
        k *= 2
    return x


def _ssd_kernel(xh_ref, bc_ref, zs_ref, dt_ref, alog_ref, nw_ref, o_ref, state_ref, *, seq):
    L = SSD_CHUNK
    n_chunks = seq // L
    gw = SSD_WIDTH // SSD_GROUPS
    per_group = SSD_HEADS // SSD_GROUPS
    state_ref[...] = jnp.zeros(state_ref.shape, F32)

    r_i = lax.broadcasted_iota(jnp.int32, (L, L), 0)
    c_i = lax.broadcasted_iota(jnp.int32, (L, L), 1)
    tril = r_i >= c_i
    lane_w = lax.broadcasted_iota(jnp.int32, (L, SSD_WIDTH), 1)
    low_half = lax.broadcasted_iota(jnp.int32, (L, LANES), 1) < HEAD_DIM
    a_neg = -jnp.exp(alog_ref[...])
    nt = (((1,), (1,)), ((), ()))

    def chunk(c, carry):
        r0 = pl.multiple_of(c * L, L)
        b_in = bc_ref[pl.ds(r0, L), 0:SSD_GROUPS * SSD_STATE]
        c_in = bc_ref[pl.ds(r0, L), SSD_GROUPS * SSD_STATE:BC_WIDTH]

        xdt = xh_ref[pl.ds(r0, L), 0:XH_SKIP]
        dt = dt_ref[pl.ds(r0, L), :]
        acum = _cumsum_rows(dt * a_neg)
        acum_t = acum.T
        acum_b = [jnp.broadcast_to(acum[:, h:h + 1], (L, LANES)) for h in range(SSD_HEADS)]
        acum_e = jnp.concatenate([jnp.where(low_half, acum_b[2 * pr], acum_b[2 * pr + 1])
                                  for pr in range(SSD_HEADS // 2)], axis=1)
        last_e = acum_e[L - 1:L, :]
        xdt_end = xdt * jnp.exp(last_e - acum_e)

        state = state_ref[...]
        new_state = state * jnp.exp(last_e)
        y_off = jnp.zeros((L, SSD_WIDTH), F32)
        cb = []
        for g in range(SSD_GROUPS):
            in_g = (lane_w >= g * gw) & (lane_w < (g + 1) * gw)
            bg = b_in[:, g * SSD_STATE:(g + 1) * SSD_STATE]
            cg = c_in[:, g * SSD_STATE:(g + 1) * SSD_STATE]
            cb.append(lax.dot_general(cg, bg, nt, preferred_element_type=F32))
            y_off = y_off + jnp.dot(cg, jnp.where(in_g[:SSD_STATE], state, 0.0).astype(BF16),
                                    preferred_element_type=F32)
            new_state = new_state + jnp.dot(bg.astype(F32).T.astype(BF16),
                                            jnp.where(in_g, xdt_end, 0.0).astype(BF16),
                                            preferred_element_type=F32)
        state_ref[...] = new_state

        y_diag = []
        for pr in range(SSD_HEADS // 2):
            xp = xdt[:, pr * LANES:(pr + 1) * LANES].astype(BF16)
            halves = []
            for h in (2 * pr, 2 * pr + 1):
                decay = jnp.exp(jnp.where(tril, acum_b[h] - acum_t[h:h + 1, :], NEG_INF))
                halves.append(jnp.dot((cb[h // per_group] * decay).astype(BF16), xp,
                                      preferred_element_type=F32))
            y_diag.append(jnp.where(low_half, halves[0], halves[1]))
        y = (jnp.concatenate(y_diag, axis=1) + xh_ref[pl.ds(r0, L), XH_SKIP:XH_WIDTH]
             + y_off * jnp.exp(acum_e))

        y = y * zs_ref[pl.ds(r0, L), :]
        y2 = y * y
        inv = jnp.zeros((L, SSD_WIDTH), F32)
        for g in range(SSD_GROUPS):
            in_g = (lane_w >= g * gw) & (lane_w < (g + 1) * gw)
            ms = jnp.sum(jnp.where(in_g, y2, 0.0), axis=-1, keepdims=True) * (1.0 / gw)
            inv = jnp.where(in_g, lax.rsqrt(ms + NORM_EPS), inv)
        o_ref[pl.ds(r0, L), :] = y * inv * nw_ref[...]
        return carry

    lax.fori_loop(0, n_chunks, chunk, 0, unroll=SSD_UNROLL)


def _ssd(xh3, bc3, zs3, dt3, alog, nw, layer):
    b, seq, _ = xh3.shape
    kern = functools.partial(_ssd_kernel, seq=seq)
    per_batch = lambda wd: pl.BlockSpec((None, seq, wd), lambda bi: (bi, 0, 0))
    return pl.pallas_call(
        kern,
        grid=(b,),
        in_specs=[per_batch(XH_WIDTH), per_batch(BC_WIDTH), per_batch(SSD_WIDTH),
                  per_batch(DT_PAD),
                  _resident((1, DT_PAD), layer), _resident((1, SSD_WIDTH), layer)],
        out_specs=per_batch(SSD_WIDTH),
        out_shape=jax.ShapeDtypeStruct((b, seq, SSD_WIDTH), F32),
        scratch_shapes=[pltpu.VMEM((SSD_STATE, SSD_WIDTH), F32)],
        compiler_params=pltpu.CompilerParams(dimension_semantics=("arbitrary",),
                                             vmem_limit_bytes=VMEM_LIMIT),
        name="ssd_mixer",
    )(xh3, bc3, zs3, dt3, alog, nw)


def _out_ffn_kernel(x_ref, ya_ref, ys_ref, yp_ref, wo_ref, g_ref, wgu_ref, wd_ref, gf_ref, o_ref,
                    *, final):
    y = jnp.concatenate([ya_ref[...], ys_ref[...], yp_ref[...]], axis=1).astype(BF16)
    x = x_ref[...] + jnp.dot(y, wo_ref[...], preferred_element_type=F32)
    h = _rmsnorm(x, g_ref[...]).astype(BF16)
    gate = jnp.dot(h, wgu_ref[:, :FFN_HIDDEN], preferred_element_type=F32)
    up = jnp.dot(h, wgu_ref[:, FFN_HIDDEN:], preferred_element_type=F32)
    a = (_silu(gate) * up).astype(BF16)
    acc = x + jnp.dot(a, wd_ref[...], preferred_element_type=F32)
    if final:
        acc = _rmsnorm(acc, gf_ref[...])
    o_ref[...] = acc


def _out_ffn(x2, ya, ys, yp, wo, g, wgu, wd, gf, final, layer):
    m = x2.shape[0]
    kern = functools.partial(_out_ffn_kernel, final=final)
    rows = lambda wd_: pl.BlockSpec((ROW_TILE, wd_), lambda i: (i, 0))
    return pl.pallas_call(
        kern,
        grid=(m // ROW_TILE,),
        in_specs=[rows(D_MODEL), rows(ATTN_WIDTH), rows(SSD_WIDTH), rows(POOL_WIDTH),
                  _resident((MIX_WIDTH, D_MODEL)), _resident((1, D_MODEL), layer),
                  _resident((D_MODEL, 2 * FFN_HIDDEN)), _resident((FFN_HIDDEN, D_MODEL)),
                  _resident((1, D_MODEL))],
        out_specs=rows(D_MODEL),
        out_shape=jax.ShapeDtypeStruct((m, D_MODEL), F32),
        compiler_params=pltpu.CompilerParams(dimension_semantics=("arbitrary",),
                                             vmem_limit_bytes=VMEM_LIMIT),
        name="out_ffn",
    )(x2, ya, ys, yp, wo, g, wgu, wd, gf)


def _pad_lanes(v, width):
    return jnp.pad(v, [(0, 0)] * (v.ndim - 1) + [(0, width - v.shape[-1])])


def kernel(x, norm_mix, w_in, conv_w, conv_b, dt_bias, a_log, d_skip, ssd_norm, pool_w, pool_scale,
           w_out, norm_ffn, w_gate_up, w_down, norm_final):
    bsz, seq, d = x.shape
    depth = w_in.shape[0]
    m = bsz * seq
    w_main = w_in[:, :, :W_IN_DT].astype(BF16)
    w_tail = jnp.concatenate([_pad_lanes(w_in[:, :, W_IN_DT:W_IN_P], DT_PAD), w_in[:, :, W_IN_P:]],
                             axis=-1).astype(BF16)
    eye = jnp.eye(len(POOL_WINDOWS), dtype=F32)
    pool_bd = (eye[None, :, None, :, None] * pool_w[:, :, :, None, :]).reshape(
        depth, POOL_WIDTH, POOL_WIDTH).astype(BF16)
    row = lambda v: v.reshape(depth, 1, v.shape[-1])
    dtb_p = row(_pad_lanes(dt_bias, DT_PAD))
    alog_p = row(_pad_lanes(a_log, DT_PAD))
    dskip_e = row(jnp.repeat(d_skip, HEAD_DIM, axis=-1))
    g_mix, g_ffn, conv_b3 = row(norm_mix), row(norm_ffn), row(conv_b)
    pool_s3, ssd_norm3 = row(pool_scale), row(ssd_norm)

    x2 = x.reshape(m, d)
    gf = norm_final.reshape(1, d)
    for l in range(depth):
        qk, v, zs, xh, bc, dt, yp, wo_b, wgu_b, wd_b = _in_proj(
            x2, g_mix, w_main, w_tail, conv_w, conv_b3, dtb_p, dskip_e, pool_bd, pool_s3,
            w_out, w_gate_up, w_down, seq, l)
        ya = _attention(qk.reshape(bsz, seq, 2 * ATTN_WIDTH), v.reshape(bsz, seq, ATTN_WIDTH))
        ys = _ssd(xh.reshape(bsz, seq, XH_WIDTH), bc.reshape(bsz, seq, BC_WIDTH),
                  zs.reshape(bsz, seq, SSD_WIDTH), dt.reshape(bsz, seq, DT_PAD),
                  alog_p, ssd_norm3, l)
        x2 = _out_ffn(x2, ya.reshape(m, ATTN_WIDTH), ys.reshape(m, SSD_WIDTH), yp,
                      wo_b, g_ffn, wgu_b, wd_b, gf, final=(l == depth - 1), layer=l)
    return x2.reshape(bsz, seq, d)
```

```python
import functools
import math

import jax
import jax.numpy as jnp
from jax import lax
from jax.experimental import pallas as pl
from jax.experimental.pallas import tpu as pltpu

F32 = jnp.float32
BF16 = jnp.bfloat16

D_MODEL = 1024
ATTN_HEADS = 6
HEAD_DIM = 64
ATTN_WIDTH = ATTN_HEADS * HEAD_DIM
MOBA_BLOCK = 256
MOBA_TOPK = 3
SSD_HEADS = 6
SSD_WIDTH = SSD_HEADS * HEAD_DIM
SSD_GROUPS = 2
SSD_STATE = 128
SSD_CONV = 4
SSD_CHUNK = 128
XBC_WIDTH = SSD_WIDTH + 2 * SSD_GROUPS * SSD_STATE
POOL_WINDOWS = (2, 4, 8, 16)
POOL_WIDTH = 256
MIX_WIDTH = 1024
FFN_HIDDEN = 2816
NORM_EPS = 1e-6
NEG_INF = -1e30

LANES = 128
SUBLANES = 8
QKV_W = 3 * ATTN_WIDTH
DT_PAD = LANES
XBC_UNITS = XBC_WIDTH // LANES
X_UNITS = SSD_WIDTH // LANES
QKV_UNITS = QKV_W // LANES
COL_Z = DT_PAD + XBC_WIDTH + QKV_W
COL_P = COL_Z + SSD_WIDTH
IN_PROJ_PAD = COL_P + POOL_WIDTH


def _mixed_unit(kind, u=0):
    if kind == "dt":
        return 0
    if kind == "x":
        return 1 + 2 * ((u - X_UNITS) % XBC_UNITS)
    return 2 + 2 * u if u < XBC_UNITS else 1 + XBC_UNITS + u


ROW_TILE = 512
SSD_UNROLL = 4
QK_LOOKAHEAD = 6
VMEM_LIMIT = 56 * 1024 * 1024


def _resident(shape, layer=None):
    nd = len(shape)
    if layer is None:
        return pl.BlockSpec(shape, lambda *_: (0,) * nd, pipeline_mode=pl.Buffered(1))
    return pl.BlockSpec((None,) + tuple(shape), lambda *_: (layer,) + (0,) * nd,
                        pipeline_mode=pl.Buffered(1))


def _rmsnorm(x, g):
    ms = jnp.mean(x * x, axis=-1, keepdims=True)
    return x * lax.rsqrt(ms + NORM_EPS) * g


def _softplus(x):
    return jnp.maximum(x, 0.0) + jnp.log1p(jnp.exp(-jnp.abs(x)))


def _silu(x):
    return x * jax.nn.sigmoid(x)


def _rows_back(x, tail, k):
    xw = jnp.concatenate([tail, x], axis=0)
    return pltpu.roll(xw, k, 0)[tail.shape[0]:, :]


def _cumsum_rows(x, period):
    pos = lax.broadcasted_iota(jnp.int32, x.shape, 0) & (period - 1)
    k = 1
    while k < period:
        x = x + jnp.where(pos >= k, pltpu.roll(x, k, 0), 0.0)
        k *= 2
    return x


def _expand_heads(cols):
    rows = cols.shape[0]
    lane = lax.broadcasted_iota(jnp.int32, (rows, LANES), 1)
    out = []
    for pair in range(SSD_HEADS // 2):
        lo = jnp.broadcast_to(cols[:, 2 * pair:2 * pair + 1], (rows, LANES))
        hi = jnp.broadcast_to(cols[:, 2 * pair + 1:2 * pair + 2], (rows, LANES))
        out.append(jnp.where(lane < HEAD_DIM, lo, hi))
    return jnp.concatenate(out, axis=1)


XH_SKIP = SSD_WIDTH
XH_WIDTH = 2 * SSD_WIDTH
BC_WIDTH = 2 * SSD_GROUPS * SSD_STATE
AB_WIDTH = SSD_HEADS * LANES


W_IN_Z = QKV_W
W_IN_XBC = W_IN_Z + SSD_WIDTH
W_IN_DT = W_IN_XBC + XBC_WIDTH
W_IN_P = W_IN_DT + SSD_HEADS
W_IN_WIDTH = W_IN_P + POOL_WIDTH


def _in_proj_kernel(x_ref, g_ref, w_main_ref, w_tail_ref, cw_ref, cb_ref, dtb_ref, alog_ref,
                    dskip_ref, pw_ref, ps_ref, wo_f_ref, wgu_f_ref, wd_f_ref,
                    qk_ref, v_ref, zs_ref, xh_ref, bc_ref, ab_ref, at_ref, yp_ref,
                    wo_b_ref, wgu_b_ref, wd_b_ref, ctail_ref, ptail_ref, w_ref, *, tiles_per_seq):
    @pl.when(pl.program_id(0) == 0)
    def _():
        w_ref[:, 0:DT_PAD] = w_tail_ref[:, 0:DT_PAD]
        for kind, lo, n in (("q", 0, QKV_UNITS), ("x", W_IN_XBC, XBC_UNITS)):
            for u in range(n):
                k = _mixed_unit(kind, u)
                w_ref[:, k * LANES:(k + 1) * LANES] = w_main_ref[:, lo + u * LANES:lo + (u + 1) * LANES]
        w_ref[:, COL_Z:COL_P] = w_main_ref[:, W_IN_Z:W_IN_XBC]
        w_ref[:, COL_P:IN_PROJ_PAD] = w_tail_ref[:, DT_PAD:]

    wo_b_ref[...] = wo_f_ref[...].astype(BF16)
    wgu_b_ref[...] = wgu_f_ref[...].astype(BF16)
    wd_b_ref[...] = wd_f_ref[...].astype(BF16)

    seq_tile = lax.rem(pl.program_id(0), tiles_per_seq)

    @pl.when(seq_tile == 0)
    def _():
        ctail_ref[...] = jnp.zeros(ctail_ref.shape, F32)
        ptail_ref[...] = jnp.zeros(ptail_ref.shape, F32)

    h = _rmsnorm(x_ref[...], g_ref[...]).astype(BF16)

    mix = jnp.dot(h, w_ref[:, 0:COL_Z], preferred_element_type=F32)
    p = jnp.dot(h, w_ref[:, COL_P:IN_PROJ_PAD], preferred_element_type=F32)
    z = jnp.dot(h, w_ref[:, COL_Z:COL_P], preferred_element_type=F32)

    def tile(kind, u=0):
        k = _mixed_unit(kind, u)
        return mix[:, k * LANES:(k + 1) * LANES]

    dt = _softplus(tile("dt") + dtb_ref[...])
    dt_e = _expand_heads(dt)
    acum = _cumsum_rows(dt * -jnp.exp(alog_ref[...]), SSD_CHUNK)
    for hd in range(SSD_HEADS):
        ab_ref[:, hd * LANES:(hd + 1) * LANES] = jnp.broadcast_to(acum[:, hd:hd + 1],
                                                                  (ROW_TILE, LANES))
    at_ref[...] = acum.T[0:SUBLANES, :]

    cw = cw_ref[...]
    n_ct = ctail_ref.shape[0]
    for u in sorted(range(XBC_UNITS), key=lambda v: _mixed_unit("x", v)):
        cols = slice(u * LANES, (u + 1) * LANES)
        xu = tile("x", u)
        conv = cb_ref[:, cols] + cw[SSD_CONV - 1:SSD_CONV, cols] * xu
        for j in range(SSD_CONV - 1):
            conv = conv + cw[j:j + 1, cols] * _rows_back(xu, ctail_ref[:, cols], SSD_CONV - 1 - j)
        ctail_ref[:, cols] = xu[ROW_TILE - n_ct:, :]
        act = _silu(conv)
        if u < X_UNITS:
            xh_ref[:, cols] = act * dt_e[:, cols]
            xh_ref[:, XH_SKIP + u * LANES:XH_SKIP + (u + 1) * LANES] = act * dskip_ref[:, cols]
        else:
            lo = (u - X_UNITS) * LANES
            bc_ref[:, lo:lo + LANES] = act.astype(BF16)
    qk_units = 2 * ATTN_WIDTH // LANES
    for u in range(QKV_UNITS):
        if u < qk_units:
            qk_ref[:, u * LANES:(u + 1) * LANES] = tile("q", u)
        else:
            v_ref[:, (u - qk_units) * LANES:(u - qk_units + 1) * LANES] = tile("q", u).astype(BF16)

    n_tail = ptail_ref.shape[0]
    win = jnp.concatenate([ptail_ref[...], p], axis=0)
    ptail_ref[...] = p[ROW_TILE - n_tail:, :]
    pos = seq_tile * ROW_TILE + lax.broadcasted_iota(jnp.int32, p.shape, 0)
    lane = lax.broadcasted_iota(jnp.int32, p.shape, 1)
    total = jnp.zeros(p.shape, F32)
    cnt = jnp.ones(p.shape, F32)
    width = 1
    for gi, w in enumerate(POOL_WINDOWS):
        while width < w:
            win = win + pltpu.roll(win, width, 0)
            width *= 2
        in_g = (lane >= gi * HEAD_DIM) & (lane < (gi + 1) * HEAD_DIM)
        total = jnp.where(in_g, win[n_tail:, :], total)
        cnt = jnp.where(in_g, jnp.minimum(pos + 1, w).astype(F32), cnt)
    d = total / cnt - p
    zs_ref[...] = _silu(z)
    yp_ref[...] = jnp.dot(d.astype(BF16), pw_ref[...], preferred_element_type=F32) * ps_ref[...]


def _slab_specs(rows, cols, steps, layer):
    bf16_rows = 2 * SUBLANES
    n_slabs = max(n for n in range(1, steps + 1)
                  if steps % n == 0 and rows % (n * bf16_rows) == 0)
    slab = rows // n_slabs
    per = steps // n_slabs
    src = pl.BlockSpec((None, slab, cols), lambda i: (layer, i // per, 0))
    dst = pl.BlockSpec((slab, cols), lambda i: (i // per, 0))
    return src, dst


def _in_proj(x2, g, w_main, w_tail, cw, cb, dtb, alog, dskip, pw, ps, wo, wgu, wd, seq, layer):
    m = x2.shape[0]
    steps = m // ROW_TILE
    rows = lambda wd_: pl.BlockSpec((ROW_TILE, wd_), lambda i: (i, 0))
    outs = [(rows(2 * ATTN_WIDTH), (m, 2 * ATTN_WIDTH), F32), (rows(ATTN_WIDTH), (m, ATTN_WIDTH), BF16),
            (rows(SSD_WIDTH), (m, SSD_WIDTH), F32), (rows(XH_WIDTH), (m, XH_WIDTH), F32),
            (rows(BC_WIDTH), (m, BC_WIDTH), BF16), (rows(AB_WIDTH), (m, AB_WIDTH), F32),
            (pl.BlockSpec((SUBLANES, ROW_TILE), lambda i: (0, i)), (SUBLANES, m), F32),
            (rows(POOL_WIDTH), (m, POOL_WIDTH), F32)]
    slabs = [_slab_specs(a.shape[1], a.shape[2], steps, layer) for a in (wo, wgu, wd)]
    kern = functools.partial(_in_proj_kernel, tiles_per_seq=seq // ROW_TILE)
    return pl.pallas_call(
        kern,
        grid=(steps,),
        in_specs=[pl.BlockSpec((ROW_TILE, D_MODEL), lambda i: (i, 0)),
                  _resident((1, D_MODEL), layer),
                  _resident((D_MODEL, W_IN_DT), layer),
                  _resident((D_MODEL, DT_PAD + POOL_WIDTH), layer),
                  _resident((SSD_CONV, XBC_WIDTH), layer), _resident((1, XBC_WIDTH), layer),
                  _resident((1, DT_PAD), layer), _resident((1, DT_PAD), layer),
                  _resident((1, SSD_WIDTH), layer),
                  _resident((POOL_WIDTH, POOL_WIDTH), layer), _resident((1, POOL_WIDTH), layer)]
        + [src for src, _ in slabs],
        out_specs=[spec for spec, _, _ in outs] + [dst for _, dst in slabs],
        out_shape=[jax.ShapeDtypeStruct(shape, dt) for _, shape, dt in outs]
        + [jax.ShapeDtypeStruct(a.shape[1:], BF16) for a in (wo, wgu, wd)],
        scratch_shapes=[pltpu.VMEM((SUBLANES, XBC_WIDTH), F32),
                        pltpu.VMEM((2 * SUBLANES, POOL_WIDTH), F32),
                        pltpu.VMEM((D_MODEL, IN_PROJ_PAD), BF16)],
        compiler_params=pltpu.CompilerParams(dimension_semantics=("arbitrary",),
                                             vmem_limit_bytes=VMEM_LIMIT),
        name="in_proj",
    )(x2, g, w_main, w_tail, cw, cb, dtb, alog, dskip, pw, ps, wo, wgu, wd)


def _moba_bias(gate_t, blk):
    nb = gate_t.shape[0]
    blk_id = lax.broadcasted_iota(jnp.int32, gate_t.shape, 0)
    valid = blk_id < blk
    g = jnp.where(valid, gate_t, NEG_INF)
    rank = jnp.zeros(gate_t.shape, jnp.int32)
    for m in range(nb):
        gm = g[m:m + 1, :]
        beats = (gm > g) | ((gm == g) & (m < blk_id))
        rank = rank + beats.astype(jnp.int32)
    sel = valid & (rank < MOBA_TOPK)
    return jnp.where(sel, 0.0, NEG_INF).astype(F32)


def _attn_kernel(q_ref, k_ref, v_ref, o_ref, kb_ref, vta_ref, vtb_ref, kmean_ref, *, seq):
    n_blocks = seq // MOBA_BLOCK
    tq = MOBA_BLOCK
    k = k_ref[...]
    kmean_ref[...] = jnp.mean(k.reshape(n_blocks, MOBA_BLOCK, LANES), axis=1)
    kb_ref[...] = k.astype(BF16)
    vt = v_ref[...].astype(F32).T
    row = lax.broadcasted_iota(jnp.int32, vt.shape, 0)
    vta_ref[...] = jnp.where(row < HEAD_DIM, vt, jnp.where(row == HEAD_DIM, 1.0, 0.0)).astype(BF16)
    vtb_ref[...] = jnp.where(row >= HEAD_DIM, vt, jnp.where(row == 0, 1.0, 0.0)).astype(BF16)

    lane = lax.broadcasted_iota(jnp.int32, (tq, LANES), 1)
    half = tq // 2
    tri = (lax.broadcasted_iota(jnp.int32, (half, half), 0)
           <= lax.broadcasted_iota(jnp.int32, (half, half), 1))
    out_row = lax.broadcasted_iota(jnp.int32, (LANES, tq), 0)
    nt = (((1,), (1,)), ((), ()))
    scale2 = HEAD_DIM ** -0.5 * math.log2(math.e)

    class Tile:
        def __init__(self, i, head):
            self.i, self.head = i, head
            self.vt_ref = vta_ref if head == 0 else vtb_ref
            self.blocks, self.m, self.acc = [], None, None

        def score_step(self, n):
            i = self.i
            if n == 0:
                q = q_ref[i * tq:(i + 1) * tq, :]
                in_head = (lane < HEAD_DIM) if self.head == 0 else (lane >= HEAD_DIM)
                qh = jnp.where(in_head, q, 0.0)
                self.qs = (qh * scale2).astype(BF16)
                self.bias = None
                if i > MOBA_TOPK:
                    gate_t = lax.dot_general(kmean_ref[...], qh, nt, preferred_element_type=F32,
                                             precision=lax.Precision.HIGHEST)
                    self.bias = _moba_bias(gate_t, i)
            s = lax.dot_general(kb_ref[n * tq:(n + 1) * tq, :], self.qs, nt,
                                preferred_element_type=F32)
            if n == i:
                top_l = jnp.where(tri, s[:half, :half], NEG_INF)
                top_r = s[:half, half:]
                bot_r = jnp.where(tri, s[half:, half:], NEG_INF)
                bm = jnp.concatenate(
                    [jnp.max(top_l, axis=0, keepdims=True),
                     jnp.maximum(jnp.max(top_r, axis=0, keepdims=True),
                                 jnp.max(bot_r, axis=0, keepdims=True))], axis=1)
                self.blocks.append((top_l, top_r, bot_r))
            else:
                bm = jnp.max(s, axis=0, keepdims=True)
                if self.bias is not None:
                    bm = bm + self.bias[n:n + 1, :]
                self.blocks.append(s)
            self.m = bm if self.m is None else jnp.maximum(self.m, bm)

        def pv_step(self, n):
            m = self.m
            if n == self.i:
                top_l, top_r, bot_r = self.blocks[n]
                m_l, m_r = m[:, :half], m[:, half:]
                p_bot_r = jnp.exp2(bot_r - m_r)
                p = jnp.concatenate(
                    [jnp.concatenate([jnp.exp2(top_l - m_l), jnp.exp2(top_r - m_r)], axis=1),
                     jnp.concatenate([jnp.zeros_like(p_bot_r), p_bot_r], axis=1)], axis=0)
            else:
                if self.bias is not None:
                    m = m - self.bias[n:n + 1, :]
                p = jnp.exp2(self.blocks[n] - m)
            part = jnp.dot(self.vt_ref[:, n * tq:(n + 1) * tq], p.astype(BF16),
                           preferred_element_type=F32)
            self.acc = part if self.acc is None else self.acc + part

    tiles = [Tile(i, head) for i in range(n_blocks) for head in (0, 1)]
    for t in tiles[:QK_LOOKAHEAD]:
        for n in range(t.i + 1):
            t.score_step(n)
    done = []
    for j, t in enumerate(tiles):
        ahead = tiles[j + QK_LOOKAHEAD] if j + QK_LOOKAHEAD < len(tiles) else None
        n_ahead = ahead.i + 1 if ahead is not None else 0
        for n in range(max(t.i + 1, n_ahead)):
            if n < n_ahead:
                ahead.score_step(n)
            if n <= t.i:
                t.pv_step(n)
        done.append(t.acc)
        if t.head == 1:
            out_t = jnp.where(out_row < HEAD_DIM, done[0] / done[0][HEAD_DIM:HEAD_DIM + 1, :],
                              done[1] / done[1][0:1, :])
            o_ref[t.i * tq:(t.i + 1) * tq, :] = out_t.T
            done = []


def _attention(qk3, v3):
    b, seq, _ = qk3.shape
    n_pairs = ATTN_WIDTH // LANES
    kern = functools.partial(_attn_kernel, seq=seq)
    return pl.pallas_call(
        kern,
        grid=(b, n_pairs),
        in_specs=[pl.BlockSpec((None, seq, LANES), lambda bi, j: (bi, 0, j)),
                  pl.BlockSpec((None, seq, LANES), lambda bi, j: (bi, 0, n_pairs + j)),
                  pl.BlockSpec((None, seq, LANES), lambda bi, j: (bi, 0, j))],
        out_specs=pl.BlockSpec((None, seq, LANES), lambda bi, j: (bi, 0, j)),
        out_shape=jax.ShapeDtypeStruct((b, seq, ATTN_WIDTH), F32),
        scratch_shapes=[pltpu.VMEM((seq, LANES), BF16),
                        pltpu.VMEM((LANES, seq), BF16),
                        pltpu.VMEM((LANES, seq), BF16),
                        pltpu.VMEM((seq // MOBA_BLOCK, LANES), F32)],
        compiler_params=pltpu.CompilerParams(dimension_semantics=("arbitrary", "arbitrary"),
                                             vmem_limit_bytes=VMEM_LIMIT),
        name="moba_attn",
    )(qk3, qk3, v3)


def _ssd_kernel(xh_ref, bc_ref, zs_ref, ab_ref, at_ref, nw_ref, o_ref, state_ref, *, seq):
    L = SSD_CHUNK
    n_chunks = seq // L
    gw = SSD_WIDTH // SSD_GROUPS
    per_group = SSD_HEADS // SSD_GROUPS
    state_ref[...] = jnp.zeros(state_ref.shape, F32)

    r_i = lax.broadcasted_iota(jnp.int32, (L, L), 0)
    c_i = lax.broadcasted_iota(jnp.int32, (L, L), 1)
    tril = r_i >= c_i
    lane_w = lax.broadcasted_iota(jnp.int32, (L, SSD_WIDTH), 1)
    low_half = lax.broadcasted_iota(jnp.int32, (L, LANES), 1) < HEAD_DIM
    nt = (((1,), (1,)), ((), ()))

    def chunk(c, carry):
        r0 = pl.multiple_of(c * L, L)
        b_in = bc_ref[pl.ds(r0, L), 0:SSD_GROUPS * SSD_STATE]
        c_in = bc_ref[pl.ds(r0, L), SSD_GROUPS * SSD_STATE:BC_WIDTH]

        xdt = xh_ref[pl.ds(r0, L), 0:XH_SKIP]
        acum_t = at_ref[:, pl.ds(r0, L)]
        acum_b = [ab_ref[pl.ds(r0, L), h * LANES:(h + 1) * LANES] for h in range(SSD_HEADS)]
        acum_e = jnp.concatenate([jnp.where(low_half, acum_b[2 * pr], acum_b[2 * pr + 1])
                                  for pr in range(SSD_HEADS // 2)], axis=1)
        last_e = acum_e[L - 1:L, :]
        xdt_end = xdt * jnp.exp(last_e - acum_e)

        state = state_ref[...]
        new_state = state * jnp.exp(last_e)
        y_off = jnp.zeros((L, SSD_WIDTH), F32)
        cb = []
        for g in range(SSD_GROUPS):
            in_g = (lane_w >= g * gw) & (lane_w < (g + 1) * gw)
            bg = b_in[:, g * SSD_STATE:(g + 1) * SSD_STATE]
            cg = c_in[:, g * SSD_STATE:(g + 1) * SSD_STATE]
            cb.append(lax.dot_general(cg, bg, nt, preferred_element_type=F32))
            y_off = y_off + jnp.dot(cg, jnp.where(in_g[:SSD_STATE], state, 0.0).astype(BF16),
                                    preferred_element_type=F32)
            new_state = new_state + jnp.dot(bg.astype(F32).T.astype(BF16),
                                            jnp.where(in_g, xdt_end, 0.0).astype(BF16),
                                            preferred_element_type=F32)
        state_ref[...] = new_state

        y_diag = []
        for pr in range(SSD_HEADS // 2):
            xp = xdt[:, pr * LANES:(pr + 1) * LANES].astype(BF16)
            halves = []
            for h in (2 * pr, 2 * pr + 1):
                decay = jnp.exp(jnp.where(tril, acum_b[h] - acum_t[h:h + 1, :], NEG_INF))
                halves.append(jnp.dot((cb[h // per_group] * decay).astype(BF16), xp,
                                      preferred_element_type=F32))
            y_diag.append(jnp.where(low_half, halves[0], halves[1]))
        y = (jnp.concatenate(y_diag, axis=1) + xh_ref[pl.ds(r0, L), XH_SKIP:XH_WIDTH]
             + y_off * jnp.exp(acum_e))

        y = y * zs_ref[pl.ds(r0, L), :]
        y2 = y * y
        inv = jnp.zeros((L, SSD_WIDTH), F32)
        for g in range(SSD_GROUPS):
            in_g = (lane_w >= g * gw) & (lane_w < (g + 1) * gw)
            ms = jnp.sum(jnp.where(in_g, y2, 0.0), axis=-1, keepdims=True) * (1.0 / gw)
            inv = jnp.where(in_g, lax.rsqrt(ms + NORM_EPS), inv)
        o_ref[pl.ds(r0, L), :] = y * inv * nw_ref[...]
        return carry

    lax.fori_loop(0, n_chunks, chunk, 0, unroll=SSD_UNROLL)


def _ssd(xh3, bc3, zs3, ab3, at, nw, layer):
    b, seq, _ = xh3.shape
    kern = functools.partial(_ssd_kernel, seq=seq)
    per_batch = lambda wd: pl.BlockSpec((None, seq, wd), lambda bi: (bi, 0, 0))
    return pl.pallas_call(
        kern,
        grid=(b,),
        in_specs=[per_batch(XH_WIDTH), per_batch(BC_WIDTH), per_batch(SSD_WIDTH),
                  per_batch(AB_WIDTH), pl.BlockSpec((SUBLANES, seq), lambda bi: (0, bi)),
                  _resident((1, SSD_WIDTH), layer)],
        out_specs=per_batch(SSD_WIDTH),
        out_shape=jax.ShapeDtypeStruct((b, seq, SSD_WIDTH), F32),
        scratch_shapes=[pltpu.VMEM((SSD_STATE, SSD_WIDTH), F32)],
        compiler_params=pltpu.CompilerParams(dimension_semantics=("arbitrary",),
                                             vmem_limit_bytes=VMEM_LIMIT),
        name="ssd_mixer",
    )(xh3, bc3, zs3, ab3, at, nw)


def _out_ffn_kernel(x_ref, ya_ref, ys_ref, yp_ref, wo_ref, g_ref, wgu_ref, wd_ref, gf_ref, o_ref,
                    *, final):
    y = jnp.concatenate([ya_ref[...], ys_ref[...], yp_ref[...]], axis=1).astype(BF16)
    x = x_ref[...] + jnp.dot(y, wo_ref[...], preferred_element_type=F32)
    h = _rmsnorm(x, g_ref[...]).astype(BF16)
    gate = jnp.dot(h, wgu_ref[:, :FFN_HIDDEN], preferred_element_type=F32)
    up = jnp.dot(h, wgu_ref[:, FFN_HIDDEN:], preferred_element_type=F32)
    a = (_silu(gate) * up).astype(BF16)
    acc = x + jnp.dot(a, wd_ref[...], preferred_element_type=F32)
    if final:
        acc = _rmsnorm(acc, gf_ref[...])
    o_ref[...] = acc


def _out_ffn(x2, ya, ys, yp, wo, g, wgu, wd, gf, final, layer):
    m = x2.shape[0]
    kern = functools.partial(_out_ffn_kernel, final=final)
    rows = lambda wd_: pl.BlockSpec((ROW_TILE, wd_), lambda i: (i, 0))
    return pl.pallas_call(
        kern,
        grid=(m // ROW_TILE,),
        in_specs=[rows(D_MODEL), rows(ATTN_WIDTH), rows(SSD_WIDTH), rows(POOL_WIDTH),
                  _resident((MIX_WIDTH, D_MODEL)), _resident((1, D_MODEL), layer),
                  _resident((D_MODEL, 2 * FFN_HIDDEN)), _resident((FFN_HIDDEN, D_MODEL)),
                  _resident((1, D_MODEL))],
        out_specs=rows(D_MODEL),
        out_shape=jax.ShapeDtypeStruct((m, D_MODEL), F32),
        compiler_params=pltpu.CompilerParams(dimension_semantics=("arbitrary",),
                                             vmem_limit_bytes=VMEM_LIMIT),
        name="out_ffn",
    )(x2, ya, ys, yp, wo, g, wgu, wd, gf)


def _pad_lanes(v, width):
    return jnp.pad(v, [(0, 0)] * (v.ndim - 1) + [(0, width - v.shape[-1])])


def kernel(x, norm_mix, w_in, conv_w, conv_b, dt_bias, a_log, d_skip, ssd_norm, pool_w, pool_scale,
           w_out, norm_ffn, w_gate_up, w_down, norm_final):
    bsz, seq, d = x.shape
    depth = w_in.shape[0]
    m = bsz * seq
    w_main = w_in[:, :, :W_IN_DT].astype(BF16)
    w_tail = jnp.concatenate([_pad_lanes(w_in[:, :, W_IN_DT:W_IN_P], DT_PAD), w_in[:, :, W_IN_P:]],
                             axis=-1).astype(BF16)
    eye = jnp.eye(len(POOL_WINDOWS), dtype=F32)
    pool_bd = (eye[None, :, None, :, None] * pool_w[:, :, :, None, :]).reshape(
        depth, POOL_WIDTH, POOL_WIDTH).astype(BF16)
    row = lambda v: v.reshape(depth, 1, v.shape[-1])
    dtb_p = row(_pad_lanes(dt_bias, DT_PAD))
    alog_p = row(_pad_lanes(a_log, DT_PAD))
    dskip_e = row(jnp.repeat(d_skip, HEAD_DIM, axis=-1))
    g_mix, g_ffn, conv_b3 = row(norm_mix), row(norm_ffn), row(conv_b)
    pool_s3, ssd_norm3 = row(pool_scale), row(ssd_norm)

    x2 = x.reshape(m, d)
    gf = norm_final.reshape(1, d)
    for l in range(depth):
        qk, v, zs, xh, bc, ab, at, yp, wo_b, wgu_b, wd_b = _in_proj(
            x2, g_mix, w_main, w_tail, conv_w, conv_b3, dtb_p, alog_p, dskip_e, pool_bd, pool_s3,
            w_out, w_gate_up, w_down, seq, l)
        ya = _attention(qk.reshape(bsz, seq, 2 * ATTN_WIDTH), v.reshape(bsz, seq, ATTN_WIDTH))
        ys = _ssd(xh.reshape(bsz, seq, XH_WIDTH), bc.reshape(bsz, seq, BC_WIDTH),
                  zs.reshape(bsz, seq, SSD_WIDTH), ab.reshape(bsz, seq, AB_WIDTH), at,
                  ssd_norm3, l)
        x2 = _out_ffn(x2, ya.reshape(m, ATTN_WIDTH), ys.reshape(m, SSD_WIDTH), yp,
                      wo_b, g_ffn, wgu_b, wd_b, gf, final=(l == depth - 1), layer=l)
    return x2.reshape(bsz, seq, d)
```

```python
import functools
import math

import jax
import jax.numpy as jnp
from jax import lax
from jax.experimental import pallas as pl
from jax.experimental.pallas import tpu as pltpu

F32 = jnp.float32
BF16 = jnp.bfloat16

D_MODEL = 1024
ATTN_HEADS = 6
HEAD_DIM = 64
ATTN_WIDTH = ATTN_HEADS * HEAD_DIM
MOBA_BLOCK = 256
MOBA_TOPK = 3
SSD_HEADS = 6
SSD_WIDTH = SSD_HEADS * HEAD_DIM
SSD_GROUPS = 2
SSD_STATE = 128
SSD_CONV = 4
SSD_CHUNK = 128
XBC_WIDTH = SSD_WIDTH + 2 * SSD_GROUPS * SSD_STATE
POOL_WINDOWS = (2, 4, 8, 16)
POOL_WIDTH = 256
MIX_WIDTH = 1024
FFN_HIDDEN = 2816
NORM_EPS = 1e-6
NEG_INF = -1e30

LANES = 128
SUBLANES = 8
QKV_W = 3 * ATTN_WIDTH
DT_PAD = LANES
XBC_UNITS = XBC_WIDTH // LANES
X_UNITS = SSD_WIDTH // LANES
QKV_UNITS = QKV_W // LANES
COL_Z = DT_PAD + XBC_WIDTH + QKV_W
COL_P = COL_Z + SSD_WIDTH
IN_PROJ_PAD = COL_P + POOL_WIDTH


def _mixed_unit(kind, u=0):
    if kind == "dt":
        return 0
    if kind == "x":
        return 1 + 2 * ((u - X_UNITS) % XBC_UNITS)
    return 2 + 2 * u if u < XBC_UNITS else 1 + XBC_UNITS + u


ROW_TILE = 512
SSD_UNROLL = 4
QK_LOOKAHEAD = 6
VMEM_LIMIT = 56 * 1024 * 1024


def _resident(shape, layer=None):
    nd = len(shape)
    if layer is None:
        return pl.BlockSpec(shape, lambda *_: (0,) * nd, pipeline_mode=pl.Buffered(1))
    return pl.BlockSpec((None,) + tuple(shape), lambda *_: (layer,) + (0,) * nd,
                        pipeline_mode=pl.Buffered(1))


def _rmsnorm(x, g):
    ms = jnp.mean(x * x, axis=-1, keepdims=True)
    return x * lax.rsqrt(ms + NORM_EPS) * g


def _softplus(x):
    return jnp.maximum(x, 0.0) + jnp.log1p(jnp.exp(-jnp.abs(x)))


def _silu(x):
    return x * jax.nn.sigmoid(x)


def _rows_back(x, tail, k):
    xw = jnp.concatenate([tail, x], axis=0)
    return pltpu.roll(xw, k, 0)[tail.shape[0]:, :]


def _cumsum_rows(x, period):
    pos = lax.broadcasted_iota(jnp.int32, x.shape, 0) & (period - 1)
    k = 1
    while k < period:
        x = x + jnp.where(pos >= k, pltpu.roll(x, k, 0), 0.0)
        k *= 2
    return x


def _expand_heads(cols):
    rows = cols.shape[0]
    lane = lax.broadcasted_iota(jnp.int32, (rows, LANES), 1)
    out = []
    for pair in range(SSD_HEADS // 2):
        lo = jnp.broadcast_to(cols[:, 2 * pair:2 * pair + 1], (rows, LANES))
        hi = jnp.broadcast_to(cols[:, 2 * pair + 1:2 * pair + 2], (rows, LANES))
        out.append(jnp.where(lane < HEAD_DIM, lo, hi))
    return jnp.concatenate(out, axis=1)


XH_SKIP = SSD_WIDTH
XH_WIDTH = 2 * SSD_WIDTH
BC_WIDTH = 2 * SSD_GROUPS * SSD_STATE
AB_WIDTH = SSD_WIDTH


W_IN_Z = QKV_W
W_IN_XBC = W_IN_Z + SSD_WIDTH
W_IN_DT = W_IN_XBC + XBC_WIDTH
W_IN_P = W_IN_DT + SSD_HEADS
W_IN_WIDTH = W_IN_P + POOL_WIDTH


def _in_proj_kernel(x_ref, g_ref, w_main_ref, w_tail_ref, cw_ref, cb_ref, dtb_ref, alog_ref,
                    dskip_ref, pw_ref, ps_ref, wo_f_ref, wgu_f_ref, wd_f_ref,
                    qk_ref, v_ref, zs_ref, xh_ref, bc_ref, ab_ref, at_ref, yp_ref,
                    wo_b_ref, wgu_b_ref, wd_b_ref, ctail_ref, ptail_ref, w_ref, *, tiles_per_seq):
    @pl.when(pl.program_id(0) == 0)
    def _():
        w_ref[:, 0:DT_PAD] = w_tail_ref[:, 0:DT_PAD]
        for kind, lo, n in (("q", 0, QKV_UNITS), ("x", W_IN_XBC, XBC_UNITS)):
            for u in range(n):
                k = _mixed_unit(kind, u)
                w_ref[:, k * LANES:(k + 1) * LANES] = w_main_ref[:, lo + u * LANES:lo + (u + 1) * LANES]
        w_ref[:, COL_Z:COL_P] = w_main_ref[:, W_IN_Z:W_IN_XBC]
        w_ref[:, COL_P:IN_PROJ_PAD] = w_tail_ref[:, DT_PAD:]

    wo_b_ref[...] = wo_f_ref[...].astype(BF16)
    wgu_b_ref[...] = wgu_f_ref[...].astype(BF16)
    wd_b_ref[...] = wd_f_ref[...].astype(BF16)

    seq_tile = lax.rem(pl.program_id(0), tiles_per_seq)

    @pl.when(seq_tile == 0)
    def _():
        ctail_ref[...] = jnp.zeros(ctail_ref.shape, F32)
        ptail_ref[...] = jnp.zeros(ptail_ref.shape, F32)

    h = _rmsnorm(x_ref[...], g_ref[...]).astype(BF16)

    mix = jnp.dot(h, w_ref[:, 0:COL_Z], preferred_element_type=F32)
    p = jnp.dot(h, w_ref[:, COL_P:IN_PROJ_PAD], preferred_element_type=F32)
    z = jnp.dot(h, w_ref[:, COL_Z:COL_P], preferred_element_type=F32)

    def tile(kind, u=0):
        k = _mixed_unit(kind, u)
        return mix[:, k * LANES:(k + 1) * LANES]

    dt = _softplus(tile("dt") + dtb_ref[...])
    dt_e = _expand_heads(dt)
    acum = _cumsum_rows(dt * -jnp.exp(alog_ref[...]), SSD_CHUNK)
    ab_ref[...] = _expand_heads(acum)
    at_ref[...] = acum.T[0:SUBLANES, :]

    cw = cw_ref[...]
    n_ct = ctail_ref.shape[0]
    for u in sorted(range(XBC_UNITS), key=lambda v: _mixed_unit("x", v)):
        cols = slice(u * LANES, (u + 1) * LANES)
        xu = tile("x", u)
        conv = cb_ref[:, cols] + cw[SSD_CONV - 1:SSD_CONV, cols] * xu
        for j in range(SSD_CONV - 1):
            conv = conv + cw[j:j + 1, cols] * _rows_back(xu, ctail_ref[:, cols], SSD_CONV - 1 - j)
        ctail_ref[:, cols] = xu[ROW_TILE - n_ct:, :]
        act = _silu(conv)
        if u < X_UNITS:
            xh_ref[:, cols] = act * dt_e[:, cols]
            xh_ref[:, XH_SKIP + u * LANES:XH_SKIP + (u + 1) * LANES] = act * dskip_ref[:, cols]
        else:
            lo = (u - X_UNITS) * LANES
            bc_ref[:, lo:lo + LANES] = act.astype(BF16)
    qk_units = 2 * ATTN_WIDTH // LANES
    for u in range(QKV_UNITS):
        if u < qk_units:
            qk_ref[:, u * LANES:(u + 1) * LANES] = tile("q", u)
        else:
            v_ref[:, (u - qk_units) * LANES:(u - qk_units + 1) * LANES] = tile("q", u).astype(BF16)

    n_tail = ptail_ref.shape[0]
    win = jnp.concatenate([ptail_ref[...], p], axis=0)
    ptail_ref[...] = p[ROW_TILE - n_tail:, :]
    pos = seq_tile * ROW_TILE + lax.broadcasted_iota(jnp.int32, p.shape, 0)
    lane = lax.broadcasted_iota(jnp.int32, p.shape, 1)
    total = jnp.zeros(p.shape, F32)
    cnt = jnp.ones(p.shape, F32)
    width = 1
    for gi, w in enumerate(POOL_WINDOWS):
        while width < w:
            win = win + pltpu.roll(win, width, 0)
            width *= 2
        in_g = (lane >= gi * HEAD_DIM) & (lane < (gi + 1) * HEAD_DIM)
        total = jnp.where(in_g, win[n_tail:, :], total)
        cnt = jnp.where(in_g, jnp.minimum(pos + 1, w).astype(F32), cnt)
    d = total / cnt - p
    zs_ref[...] = _silu(z)
    yp_ref[...] = (jnp.dot(d.astype(BF16), pw_ref[...], preferred_element_type=F32)
                   * ps_ref[...]).astype(BF16)


def _slab_specs(rows, cols, steps, layer):
    bf16_rows = 2 * SUBLANES
    n_slabs = max(n for n in range(1, steps + 1)
                  if steps % n == 0 and rows % (n * bf16_rows) == 0)
    slab = rows // n_slabs
    per = steps // n_slabs
    src = pl.BlockSpec((None, slab, cols), lambda i: (layer, i // per, 0))
    dst = pl.BlockSpec((slab, cols), lambda i: (i // per, 0))
    return src, dst


def _in_proj(x2, g, w_main, w_tail, cw, cb, dtb, alog, dskip, pw, ps, wo, wgu, wd, seq, layer):
    m = x2.shape[0]
    steps = m // ROW_TILE
    rows = lambda wd_: pl.BlockSpec((ROW_TILE, wd_), lambda i: (i, 0))
    outs = [(rows(2 * ATTN_WIDTH), (m, 2 * ATTN_WIDTH), F32), (rows(ATTN_WIDTH), (m, ATTN_WIDTH), BF16),
            (rows(SSD_WIDTH), (m, SSD_WIDTH), F32), (rows(XH_WIDTH), (m, XH_WIDTH), F32),
            (rows(BC_WIDTH), (m, BC_WIDTH), BF16), (rows(AB_WIDTH), (m, AB_WIDTH), F32),
            (pl.BlockSpec((SUBLANES, ROW_TILE), lambda i: (0, i)), (SUBLANES, m), F32),
            (rows(POOL_WIDTH), (m, POOL_WIDTH), BF16)]
    slabs = [_slab_specs(a.shape[1], a.shape[2], steps, layer) for a in (wo, wgu, wd)]
    kern = functools.partial(_in_proj_kernel, tiles_per_seq=seq // ROW_TILE)
    return pl.pallas_call(
        kern,
        grid=(steps,),
        in_specs=[pl.BlockSpec((ROW_TILE, D_MODEL), lambda i: (i, 0)),
                  _resident((1, D_MODEL), layer),
                  _resident((D_MODEL, W_IN_DT), layer),
                  _resident((D_MODEL, DT_PAD + POOL_WIDTH), layer),
                  _resident((SSD_CONV, XBC_WIDTH), layer), _resident((1, XBC_WIDTH), layer),
                  _resident((1, DT_PAD), layer), _resident((1, DT_PAD), layer),
                  _resident((1, SSD_WIDTH), layer),
                  _resident((POOL_WIDTH, POOL_WIDTH), layer), _resident((1, POOL_WIDTH), layer)]
        + [src for src, _ in slabs],
        out_specs=[spec for spec, _, _ in outs] + [dst for _, dst in slabs],
        out_shape=[jax.ShapeDtypeStruct(shape, dt) for _, shape, dt in outs]
        + [jax.ShapeDtypeStruct(a.shape[1:], BF16) for a in (wo, wgu, wd)],
        scratch_shapes=[pltpu.VMEM((SUBLANES, XBC_WIDTH), F32),
                        pltpu.VMEM((2 * SUBLANES, POOL_WIDTH), F32),
                        pltpu.VMEM((D_MODEL, IN_PROJ_PAD), BF16)],
        compiler_params=pltpu.CompilerParams(dimension_semantics=("arbitrary",),
                                             vmem_limit_bytes=VMEM_LIMIT),
        name="in_proj",
    )(x2, g, w_main, w_tail, cw, cb, dtb, alog, dskip, pw, ps, wo, wgu, wd)


def _moba_bias(gate_t, blk):
    nb = gate_t.shape[0]
    blk_id = lax.broadcasted_iota(jnp.int32, gate_t.shape, 0)
    valid = blk_id < blk
    g = jnp.where(valid, gate_t, NEG_INF)
    rank = jnp.zeros(gate_t.shape, jnp.int32)
    for m in range(nb):
        gm = g[m:m + 1, :]
        beats = (gm > g) | ((gm == g) & (m < blk_id))
        rank = rank + beats.astype(jnp.int32)
    sel = valid & (rank < MOBA_TOPK)
    return jnp.where(sel, 0.0, NEG_INF).astype(F32)


def _attn_kernel(q_ref, k_ref, v_ref, o_ref, kb_ref, vta_ref, vtb_ref, kmean_ref, *, seq):
    n_blocks = seq // MOBA_BLOCK
    tq = MOBA_BLOCK
    k = k_ref[...]
    kmean_ref[...] = jnp.mean(k.reshape(n_blocks, MOBA_BLOCK, LANES), axis=1)
    kb_ref[...] = k.astype(BF16)
    vt = v_ref[...].astype(F32).T
    row = lax.broadcasted_iota(jnp.int32, vt.shape, 0)
    vta_ref[...] = jnp.where(row < HEAD_DIM, vt, jnp.where(row == HEAD_DIM, 1.0, 0.0)).astype(BF16)
    vtb_ref[...] = jnp.where(row >= HEAD_DIM, vt, jnp.where(row == 0, 1.0, 0.0)).astype(BF16)

    lane = lax.broadcasted_iota(jnp.int32, (tq, LANES), 1)
    half = tq // 2
    tri = (lax.broadcasted_iota(jnp.int32, (half, half), 0)
           <= lax.broadcasted_iota(jnp.int32, (half, half), 1))
    out_row = lax.broadcasted_iota(jnp.int32, (LANES, tq), 0)
    nt = (((1,), (1,)), ((), ()))
    scale2 = HEAD_DIM ** -0.5 * math.log2(math.e)

    class Tile:
        def __init__(self, i, head):
            self.i, self.head = i, head
            self.vt_ref = vta_ref if head == 0 else vtb_ref
            self.blocks, self.m, self.acc = [], None, None

        def score_step(self, n):
            i = self.i
            if n == 0:
                q = q_ref[i * tq:(i + 1) * tq, :]
                in_head = (lane < HEAD_DIM) if self.head == 0 else (lane >= HEAD_DIM)
                qh = jnp.where(in_head, q, 0.0)
                self.qs = (qh * scale2).astype(BF16)
                self.bias = None
                if i > MOBA_TOPK:
                    gate_t = lax.dot_general(kmean_ref[...], qh, nt, preferred_element_type=F32,
                                             precision=lax.Precision.HIGHEST)
                    self.bias = _moba_bias(gate_t, i)
            s = lax.dot_general(kb_ref[n * tq:(n + 1) * tq, :], self.qs, nt,
                                preferred_element_type=F32)
            if n == i:
                top_l = jnp.where(tri, s[:half, :half], NEG_INF)
                top_r = s[:half, half:]
                bot_r = jnp.where(tri, s[half:, half:], NEG_INF)
                bm = jnp.concatenate(
                    [jnp.max(top_l, axis=0, keepdims=True),
                     jnp.maximum(jnp.max(top_r, axis=0, keepdims=True),
                                 jnp.max(bot_r, axis=0, keepdims=True))], axis=1)
                self.blocks.append((top_l, top_r, bot_r))
            else:
                bm = jnp.max(s, axis=0, keepdims=True)
                if self.bias is not None:
                    bm = bm + self.bias[n:n + 1, :]
                self.blocks.append(s)
            self.m = bm if self.m is None else jnp.maximum(self.m, bm)

        def pv_step(self, n):
            m = self.m
            if n == self.i:
                top_l, top_r, bot_r = self.blocks[n]
                m_l, m_r = m[:, :half], m[:, half:]
                p_bot_r = jnp.exp2(bot_r - m_r)
                p = jnp.concatenate(
                    [jnp.concatenate([jnp.exp2(top_l - m_l), jnp.exp2(top_r - m_r)], axis=1),
                     jnp.concatenate([jnp.zeros_like(p_bot_r), p_bot_r], axis=1)], axis=0)
            else:
                if self.bias is not None:
                    m = m - self.bias[n:n + 1, :]
                p = jnp.exp2(self.blocks[n] - m)
            part = jnp.dot(self.vt_ref[:, n * tq:(n + 1) * tq], p.astype(BF16),
                           preferred_element_type=F32)
            self.acc = part if self.acc is None else self.acc + part

    tiles = [Tile(i, head) for i in range(n_blocks) for head in (0, 1)]
    for t in tiles[:QK_LOOKAHEAD]:
        for n in range(t.i + 1):
            t.score_step(n)
    done = []
    for j, t in enumerate(tiles):
        ahead = tiles[j + QK_LOOKAHEAD] if j + QK_LOOKAHEAD < len(tiles) else None
        n_ahead = ahead.i + 1 if ahead is not None else 0
        for n in range(max(t.i + 1, n_ahead)):
            if n < n_ahead:
                ahead.score_step(n)
            if n <= t.i:
                t.pv_step(n)
        done.append(t.acc)
        if t.head == 1:
            out_t = jnp.where(out_row < HEAD_DIM, done[0] / done[0][HEAD_DIM:HEAD_DIM + 1, :],
                              done[1] / done[1][0:1, :])
            o_ref[t.i * tq:(t.i + 1) * tq, :] = out_t.T.astype(BF16)
            done = []


def _attention(qk3, v3):
    b, seq, _ = qk3.shape
    n_pairs = ATTN_WIDTH // LANES
    kern = functools.partial(_attn_kernel, seq=seq)
    return pl.pallas_call(
        kern,
        grid=(b, n_pairs),
        in_specs=[pl.BlockSpec((None, seq, LANES), lambda bi, j: (bi, 0, j)),
                  pl.BlockSpec((None, seq, LANES), lambda bi, j: (bi, 0, n_pairs + j)),
                  pl.BlockSpec((None, seq, LANES), lambda bi, j: (bi, 0, j))],
        out_specs=pl.BlockSpec((None, seq, LANES), lambda bi, j: (bi, 0, j)),
        out_shape=jax.ShapeDtypeStruct((b, seq, ATTN_WIDTH), BF16),
        scratch_shapes=[pltpu.VMEM((seq, LANES), BF16),
                        pltpu.VMEM((LANES, seq), BF16),
                        pltpu.VMEM((LANES, seq), BF16),
                        pltpu.VMEM((seq // MOBA_BLOCK, LANES), F32)],
        compiler_params=pltpu.CompilerParams(dimension_semantics=("arbitrary", "arbitrary"),
                                             vmem_limit_bytes=VMEM_LIMIT),
        name="moba_attn",
    )(qk3, qk3, v3)


def _ssd_kernel(xh_ref, bc_ref, zs_ref, ab_ref, at_ref, nw_ref, o_ref, state_ref, *, seq):
    L = SSD_CHUNK
    n_chunks = seq // L
    gw = SSD_WIDTH // SSD_GROUPS
    per_group = SSD_HEADS // SSD_GROUPS
    state_ref[...] = jnp.zeros(state_ref.shape, F32)

    r_i = lax.broadcasted_iota(jnp.int32, (L, L), 0)
    c_i = lax.broadcasted_iota(jnp.int32, (L, L), 1)
    tril = r_i >= c_i
    lane_w = lax.broadcasted_iota(jnp.int32, (L, SSD_WIDTH), 1)
    low_half = lax.broadcasted_iota(jnp.int32, (L, LANES), 1) < HEAD_DIM
    nt = (((1,), (1,)), ((), ()))

    def chunk(c, carry):
        r0 = pl.multiple_of(c * L, L)
        b_in = bc_ref[pl.ds(r0, L), 0:SSD_GROUPS * SSD_STATE]
        c_in = bc_ref[pl.ds(r0, L), SSD_GROUPS * SSD_STATE:BC_WIDTH]

        xdt = xh_ref[pl.ds(r0, L), 0:XH_SKIP]
        acum_t = at_ref[:, pl.ds(r0, L)]
        acum_e = ab_ref[pl.ds(r0, L), :]
        acum_b = []
        for pr in range(SSD_HEADS // 2):
            tile = acum_e[:, pr * LANES:(pr + 1) * LANES]
            swapped = pltpu.roll(tile, HEAD_DIM, 1)
            acum_b += [jnp.where(low_half, tile, swapped), jnp.where(low_half, swapped, tile)]
        last_e = acum_e[L - 1:L, :]
        xdt_end = xdt * jnp.exp(last_e - acum_e)

        state = state_ref[...]
        new_state = state * jnp.exp(last_e)
        y_off = jnp.zeros((L, SSD_WIDTH), F32)
        cb = []
        for g in range(SSD_GROUPS):
            in_g = (lane_w >= g * gw) & (lane_w < (g + 1) * gw)
            bg = b_in[:, g * SSD_STATE:(g + 1) * SSD_STATE]
            cg = c_in[:, g * SSD_STATE:(g + 1) * SSD_STATE]
            cb.append(lax.dot_general(cg, bg, nt, preferred_element_type=F32))
            y_off = y_off + jnp.dot(cg, jnp.where(in_g[:SSD_STATE], state, 0.0).astype(BF16),
                                    preferred_element_type=F32)
            new_state = new_state + jnp.dot(bg.astype(F32).T.astype(BF16),
                                            jnp.where(in_g, xdt_end, 0.0).astype(BF16),
                                            preferred_element_type=F32)
        state_ref[...] = new_state

        y_diag = []
        for pr in range(SSD_HEADS // 2):
            xp = xdt[:, pr * LANES:(pr + 1) * LANES].astype(BF16)
            halves = []
            for h in (2 * pr, 2 * pr + 1):
                decay = jnp.exp(jnp.where(tril, acum_b[h] - acum_t[h:h + 1, :], NEG_INF))
                halves.append(jnp.dot((cb[h // per_group] * decay).astype(BF16), xp,
                                      preferred_element_type=F32))
            y_diag.append(jnp.where(low_half, halves[0], halves[1]))
        y = (jnp.concatenate(y_diag, axis=1) + xh_ref[pl.ds(r0, L), XH_SKIP:XH_WIDTH]
             + y_off * jnp.exp(acum_e))

        y = y * zs_ref[pl.ds(r0, L), :]
        y2 = y * y
        inv = jnp.zeros((L, SSD_WIDTH), F32)
        for g in range(SSD_GROUPS):
            in_g = (lane_w >= g * gw) & (lane_w < (g + 1) * gw)
            ms = jnp.sum(jnp.where(in_g, y2, 0.0), axis=-1, keepdims=True) * (1.0 / gw)
            inv = jnp.where(in_g, lax.rsqrt(ms + NORM_EPS), inv)
        o_ref[pl.ds(r0, L), :] = (y * inv * nw_ref[...]).astype(BF16)
        return carry

    lax.fori_loop(0, n_chunks, chunk, 0, unroll=SSD_UNROLL)


def _ssd(xh3, bc3, zs3, ab3, at, nw, layer):
    b, seq, _ = xh3.shape
    kern = functools.partial(_ssd_kernel, seq=seq)
    per_batch = lambda wd: pl.BlockSpec((None, seq, wd), lambda bi: (bi, 0, 0))
    return pl.pallas_call(
        kern,
        grid=(b,),
        in_specs=[per_batch(XH_WIDTH), per_batch(BC_WIDTH), per_batch(SSD_WIDTH),
                  per_batch(AB_WIDTH), pl.BlockSpec((SUBLANES, seq), lambda bi: (0, bi)),
                  _resident((1, SSD_WIDTH), layer)],
        out_specs=per_batch(SSD_WIDTH),
        out_shape=jax.ShapeDtypeStruct((b, seq, SSD_WIDTH), BF16),
        scratch_shapes=[pltpu.VMEM((SSD_STATE, SSD_WIDTH), F32)],
        compiler_params=pltpu.CompilerParams(dimension_semantics=("arbitrary",),
                                             vmem_limit_bytes=VMEM_LIMIT),
        name="ssd_mixer",
    )(xh3, bc3, zs3, ab3, at, nw)


def _out_ffn_kernel(x_ref, ya_ref, ys_ref, yp_ref, wo_ref, g_ref, wgu_ref, wd_ref, gf_ref, o_ref,
                    *, final):
    y = jnp.concatenate([ya_ref[...], ys_ref[...], yp_ref[...]], axis=1)
    x = x_ref[...] + jnp.dot(y, wo_ref[...], preferred_element_type=F32)
    h = _rmsnorm(x, g_ref[...]).astype(BF16)
    gate = jnp.dot(h, wgu_ref[:, :FFN_HIDDEN], preferred_element_type=F32)
    up = jnp.dot(h, wgu_ref[:, FFN_HIDDEN:], preferred_element_type=F32)
    a = (_silu(gate) * up).astype(BF16)
    acc = x + jnp.dot(a, wd_ref[...], preferred_element_type=F32)
    if final:
        acc = _rmsnorm(acc, gf_ref[...])
    o_ref[...] = acc


def _out_ffn(x2, ya, ys, yp, wo, g, wgu, wd, gf, final, layer):
    m = x2.shape[0]
    kern = functools.partial(_out_ffn_kernel, final=final)
    rows = lambda wd_: pl.BlockSpec((ROW_TILE, wd_), lambda i: (i, 0))
    return pl.pallas_call(
        kern,
        grid=(m // ROW_TILE,),
        in_specs=[rows(D_MODEL), rows(ATTN_WIDTH), rows(SSD_WIDTH), rows(POOL_WIDTH),
                  _resident((MIX_WIDTH, D_MODEL)), _resident((1, D_MODEL), layer),
                  _resident((D_MODEL, 2 * FFN_HIDDEN)), _resident((FFN_HIDDEN, D_MODEL)),
                  _resident((1, D_MODEL))],
        out_specs=rows(D_MODEL),
        out_shape=jax.ShapeDtypeStruct((m, D_MODEL), F32),
        compiler_params=pltpu.CompilerParams(dimension_semantics=("arbitrary",),
                                             vmem_limit_bytes=VMEM_LIMIT),
        name="out_ffn",
    )(x2, ya, ys, yp, wo, g, wgu, wd, gf)


def _pad_lanes(v, width):
    return jnp.pad(v, [(0, 0)] * (v.ndim - 1) + [(0, width - v.shape[-1])])


def kernel(x, norm_mix, w_in, conv_w, conv_b, dt_bias, a_log, d_skip, ssd_norm, pool_w, pool_scale,
           w_out, norm_ffn, w_gate_up, w_down, norm_final):
    bsz, seq, d = x.shape
    depth = w_in.shape[0]
    m = bsz * seq
    w_main = w_in[:, :, :W_IN_DT].astype(BF16)
    w_tail = jnp.concatenate([_pad_lanes(w_in[:, :, W_IN_DT:W_IN_P], DT_PAD), w_in[:, :, W_IN_P:]],
                             axis=-1).astype(BF16)
    eye = jnp.eye(len(POOL_WINDOWS), dtype=F32)
    pool_bd = (eye[None, :, None, :, None] * pool_w[:, :, :, None, :]).reshape(
        depth, POOL_WIDTH, POOL_WIDTH).astype(BF16)
    row = lambda v: v.reshape(depth, 1, v.shape[-1])
    dtb_p = row(_pad_lanes(dt_bias, DT_PAD))
    alog_p = row(_pad_lanes(a_log, DT_PAD))
    dskip_e = row(jnp.repeat(d_skip, HEAD_DIM, axis=-1))
    g_mix, g_ffn, conv_b3 = row(norm_mix), row(norm_ffn), row(conv_b)
    pool_s3, ssd_norm3 = row(pool_scale), row(ssd_norm)

    x2 = x.reshape(m, d)
    gf = norm_final.reshape(1, d)
    for l in range(depth):
        qk, v, zs, xh, bc, ab, at, yp, wo_b, wgu_b, wd_b = _in_proj(
            x2, g_mix, w_main, w_tail, conv_w, conv_b3, dtb_p, alog_p, dskip_e, pool_bd, pool_s3,
            w_out, w_gate_up, w_down, seq, l)
        ya = _attention(qk.reshape(bsz, seq, 2 * ATTN_WIDTH), v.reshape(bsz, seq, ATTN_WIDTH))
        ys = _ssd(xh.reshape(bsz, seq, XH_WIDTH), bc.reshape(bsz, seq, BC_WIDTH),
                  zs.reshape(bsz, seq, SSD_WIDTH), ab.reshape(bsz, seq, AB_WIDTH), at,
                  ssd_norm3, l)
        x2 = _out_ffn(x2, ya.reshape(m, ATTN_WIDTH), ys.reshape(m, SSD_WIDTH), yp,
                      wo_b, g_ffn, wgu_b, wd_b, gf, final=(l == depth - 1), layer=l)
    return x2.reshape(bsz, seq, d)
```
